```python
import jax, jax.numpy as jnp
from jax import lax
import numpy as np

D_MODEL = 4096
BATCH = 4
SEQ = 2048
DEPTH = 4
DEC_BATCH = 8
DEC_SEQ = 4
PAST_LEN = 8192
PAGE_SIZE = 128

SB_HEADS = 16
SB_HEAD_DIM = D_MODEL // 32
SB_WIDTH = SB_HEADS * SB_HEAD_DIM
SB_BIAS_INIT = -5.0
RET_HEADS = 8
RET_HEAD_DIM = D_MODEL // 16
RET_WIDTH = RET_HEADS * RET_HEAD_DIM
Q_BLOCK = 128
RET_CHUNK = 128
D_FF = -(-8 * D_MODEL // (3 * 256)) * 256
PLE_DIM = 256
ALPHA = (2 * DEPTH) ** 0.25
BETA = (8 * DEPTH) ** -0.25
LN_EPS = 1e-5
GN_EPS = 1e-6
ROPE_BASE = 10000.0
IN_SPLITS = [SB_WIDTH] * 3 + [RET_WIDTH] * 4 + [D_MODEL] * 2
N_IN = sum(IN_SPLITS)

kernel_name = "stickbreak_retention_gated_deepnorm_step"


def layer_norm(x, g, b, eps):
    xf = x.astype(jnp.float32)
    mu = jnp.mean(xf, axis=-1, keepdims=True)
    var = jnp.mean(jnp.square(xf - mu), axis=-1, keepdims=True)
    y = (xf - mu) * lax.rsqrt(var + eps) * g.astype(jnp.float32) + b.astype(jnp.float32)
    return y.astype(x.dtype)


def rotary(x, pos):
    d = x.shape[-1]
    theta = ROPE_BASE ** (-jnp.arange(0, d, 2, dtype=jnp.float32) / d)
    ang = pos.astype(jnp.float32)[:, None] * theta[None, :]
    cos = jnp.cos(ang)[None, :, None, :]
    sin = jnp.sin(ang)[None, :, None, :]
    xf = x.astype(jnp.float32)
    x1, x2 = xf[..., : d // 2], xf[..., d // 2:]
    return jnp.concatenate([x1 * cos - x2 * sin, x1 * sin + x2 * cos], axis=-1)


def project_inputs(x, w_in, pos):
    B, T, _ = x.shape
    h = jnp.einsum('btd,dn->btn', x, w_in)
    cut = [int(c) for c in np.cumsum(IN_SPLITS)[:-1]]
    qa, ka, va, qb, kb, vb, gsw, ga, gb = jnp.split(h, cut, axis=-1)
    sb = lambda a: a.reshape(B, T, SB_HEADS, SB_HEAD_DIM)
    rt = lambda a: a.reshape(B, T, RET_HEADS, RET_HEAD_DIM)
    qr = rotary(rt(qb), pos)
    kr = rotary(rt(kb), pos) * (RET_HEAD_DIM ** -0.5)
    return sb(qa), sb(ka), sb(va), qr, kr, rt(vb), gsw, ga, gb


def stick_breaking(q, k, v, q_pos, bias):
    z = jnp.einsum('bqhd,bkhd->bhqk', q.astype(jnp.float32), k.astype(jnp.float32)) * (q.shape[-1] ** -0.5)
    z = z + bias.astype(jnp.float32)[None, :, None, None]
    k_pos = jnp.arange(k.shape[1])
    causal = k_pos[None, :] < q_pos[:, None]
    log_keep = jnp.where(causal, -jax.nn.softplus(z), 0.0)
    between = lax.cumsum(log_keep, axis=3, reverse=True) - log_keep
    a = jnp.where(causal, jnp.exp(jax.nn.log_sigmoid(z) + between), 0.0)
    return jnp.einsum('bhqk,bkhd->bqhd', a, v.astype(jnp.float32))


def stick_breaking_prompt(q, k, v, bias):
    B, T, H, d = q.shape
    nb = T // Q_BLOCK
    qb = q.reshape(B, nb, Q_BLOCK, H, d).swapaxes(0, 1)
    pos = jnp.arange(T).reshape(nb, Q_BLOCK)
    o = lax.map(lambda a: stick_breaking(a[0], k, v, a[1], bias), (qb, pos))
    return o.swapaxes(0, 1).reshape(B, T, H, d)


def retention_log_decay():
    return jnp.log1p(-jnp.exp2(-5.0 - jnp.arange(RET_HEADS, dtype=jnp.float32)))


def retention_chunk(q, k, v, s0):
    C = q.shape[1]
    lg = retention_log_decay()
    idx = jnp.arange(C, dtype=jnp.float32)
    diff = idx[:, None] - idx[None, :]
    dmask = jnp.where(diff >= 0, jnp.exp(lg[:, None, None] * jnp.maximum(diff, 0.0)), 0.0)
    inner = jnp.einsum('bihd,bjhd->bhij', q, k) * dmask[None]
    o_intra = jnp.einsum('bhij,bjhe->bihe', inner, v)
    q_decay = jnp.exp(lg[:, None] * (idx[None, :] + 1.0))
    o_cross = jnp.einsum('bihd,bhde->bihe', q, s0) * q_decay.T[None, :, :, None]
    k_decay = jnp.exp(lg[:, None] * (C - 1.0 - idx[None, :]))
    s_new = jnp.exp(lg * C)[None, :, None, None] * s0 + jnp.einsum('bjhd,hj,bjhe->bhde', k, k_decay, v)
    return o_intra + o_cross, s_new


def retention(q, k, v, s0):
    B, T, H, dk = q.shape
    C = min(RET_CHUNK, T)
    nc = T // C
    ch = lambda a: a.astype(jnp.float32).reshape(B, nc, C, H, a.shape[-1]).swapaxes(0, 1)

    def step(s, qkv):
        o, s_new = retention_chunk(qkv[0], qkv[1], qkv[2], s)
        return s_new, o

    s_final, o = lax.scan(step, s0.astype(jnp.float32), (ch(q), ch(k), ch(v)))
    return o.swapaxes(0, 1).reshape(B, T, H, v.shape[-1]), s_final


def retention_output(o, gsw, gn_g, gn_b):
    B, T = o.shape[:2]
    mu = jnp.mean(o, axis=-1, keepdims=True)
    var = jnp.mean(jnp.square(o - mu), axis=-1, keepdims=True)
    y = ((o - mu) * lax.rsqrt(var + GN_EPS)).reshape(B, T, RET_WIDTH)
    y = y * gn_g.astype(jnp.float32) + gn_b.astype(jnp.float32)
    return (jax.nn.silu(gsw.astype(jnp.float32)) * y).astype(gsw.dtype)


def finish_layer(x, oa, yb, ga, gb, p, w_proj_a, w_proj_b, w_out, ln1_g, ln1_b,
                 w_ffn_in, w_ffn_out, w_ple_gate, w_ple_proj, ln2_g, ln2_b):
    B, T, _ = x.shape
    branch_a = jnp.einsum('btw,wd->btd', oa.reshape(B, T, SB_WIDTH).astype(x.dtype), w_proj_a)
    branch_b = jnp.einsum('btw,wd->btd', yb, w_proj_b)
    merged = jax.nn.sigmoid(ga) * branch_a + jax.nn.sigmoid(gb) * branch_b
    x1 = layer_norm(ALPHA * x + jnp.einsum('btd,de->bte', merged, w_out), ln1_g, ln1_b, LN_EPS)
    gu = jnp.einsum('btd,df->btf', x1, w_ffn_in)
    ffn = jnp.einsum('btf,fd->btd', jax.nn.silu(gu[..., :D_FF]) * gu[..., D_FF:], w_ffn_out)
    ple = jax.nn.sigmoid(jnp.einsum('btd,de->bte', x1, w_ple_gate)) * jnp.einsum('btp,pd->btd', p.astype(x.dtype), w_ple_proj)
    return layer_norm(ALPHA * x1 + ffn + ple, ln2_g, ln2_b, LN_EPS)


def setup_inputs(seed: int = 0) -> dict:
    key = jax.random.key(seed)
    ks = jax.random.split(key, 24)
    n_pages = PAST_LEN // PAGE_SIZE
    n_used = DEC_BATCH * n_pages
    n_pool = (n_used * 5) // 4
    nrm = lambda k, shape, s: jax.random.normal(k, shape, jnp.float32) * s
    page_table = jax.random.permutation(ks[0], n_pool)[:n_used].reshape(DEC_BATCH, n_pages).astype(jnp.int32)
    return {
        'x_prompt': nrm(ks[1], (BATCH, SEQ, D_MODEL), 1.0),
        'x_sample': nrm(ks[2], (DEC_BATCH, DEC_SEQ, D_MODEL), 1.0),
        'cache_sb_k': nrm(ks[3], (DEPTH, n_pool, PAGE_SIZE, SB_HEADS, SB_HEAD_DIM), 1.0),
        'cache_sb_v': nrm(ks[4], (DEPTH, n_pool, PAGE_SIZE, SB_HEADS, SB_HEAD_DIM), 1.0),
        'state_ret': nrm(ks[5], (DEPTH, DEC_BATCH, RET_HEADS, RET_HEAD_DIM, RET_HEAD_DIM), 0.5),
        'page_table': page_table,
        'p_prompt': nrm(ks[6], (DEPTH, BATCH, SEQ, PLE_DIM), 1.0),
        'p_sample': nrm(ks[7], (DEPTH, DEC_BATCH, DEC_SEQ, PLE_DIM), 1.0),
        'w_in': nrm(ks[8], (DEPTH, D_MODEL, N_IN), D_MODEL ** -0.5),
        'b_sb': SB_BIAS_INIT + nrm(ks[22], (DEPTH, SB_HEADS), 0.1),
        'gn_g': 1.0 + nrm(ks[9], (DEPTH, RET_WIDTH), 0.02),
        'gn_b': nrm(ks[10], (DEPTH, RET_WIDTH), 0.02),
        'w_proj_a': nrm(ks[11], (DEPTH, SB_WIDTH, D_MODEL), BETA * SB_WIDTH ** -0.5),
        'w_proj_b': nrm(ks[12], (DEPTH, RET_WIDTH, D_MODEL), BETA * RET_WIDTH ** -0.5),
        'w_out': nrm(ks[13], (DEPTH, D_MODEL, D_MODEL), BETA * D_MODEL ** -0.5),
        'ln1_g': 1.0 + nrm(ks[14], (DEPTH, D_MODEL), 0.02),
        'ln1_b': nrm(ks[15], (DEPTH, D_MODEL), 0.02),
        'w_ffn_in': nrm(ks[16], (DEPTH, D_MODEL, 2 * D_FF), D_MODEL ** -0.5),
        'w_ffn_out': nrm(ks[17], (DEPTH, D_FF, D_MODEL), BETA * D_FF ** -0.5),
        'w_ple_gate': nrm(ks[18], (DEPTH, D_MODEL, D_MODEL), D_MODEL ** -0.5),
        'w_ple_proj': nrm(ks[19], (DEPTH, PLE_DIM, D_MODEL), BETA * PLE_DIM ** -0.5),
        'ln2_g': 1.0 + nrm(ks[20], (DEPTH, D_MODEL), 0.02),
        'ln2_b': nrm(ks[21], (DEPTH, D_MODEL), 0.02),
    }


def reference(x_prompt, x_sample, cache_sb_k, cache_sb_v, state_ret, page_table, p_prompt, p_sample,
              w_in, b_sb, gn_g, gn_b, w_proj_a, w_proj_b, w_out, ln1_g, ln1_b,
              w_ffn_in, w_ffn_out, w_ple_gate, w_ple_proj, ln2_g, ln2_b):
    xp, xs = x_prompt, x_sample
    Bp, Tp, _ = xp.shape
    Bs, Ts, _ = xs.shape
    past = page_table.shape[1] * PAGE_SIZE
    pos_p = jnp.arange(Tp)
    pos_s = past + jnp.arange(Ts)
    kp_l, vp_l, sp_l, ks_l, vs_l, ss_l = [], [], [], [], [], []
    for i in range(DEPTH):
        lw = (w_proj_a[i], w_proj_b[i], w_out[i], ln1_g[i], ln1_b[i], w_ffn_in[i], w_ffn_out[i],
              w_ple_gate[i], w_ple_proj[i], ln2_g[i], ln2_b[i])
        qa, ka, va, qb, kb, vb, gsw, ga, gb = project_inputs(xp, w_in[i], pos_p)
        oa = stick_breaking_prompt(qa, ka, va, b_sb[i])
        ob, s_p = retention(qb, kb, vb, jnp.zeros((Bp, RET_HEADS, RET_HEAD_DIM, RET_HEAD_DIM), jnp.float32))
        yb = retention_output(ob, gsw, gn_g[i], gn_b[i])
        xp = finish_layer(xp, oa, yb, ga, gb, p_prompt[i], *lw)
        kp_l.append(ka)
        vp_l.append(va)
        sp_l.append(s_p)
        qa, ka, va, qb, kb, vb, gsw, ga, gb = project_inputs(xs, w_in[i], pos_s)
        past_k = cache_sb_k[i][page_table].reshape(Bs, past, SB_HEADS, SB_HEAD_DIM)
        past_v = cache_sb_v[i][page_table].reshape(Bs, past, SB_HEADS, SB_HEAD_DIM)
        k_all = jnp.concatenate([past_k, ka.astype(past_k.dtype)], axis=1)
        v_all = jnp.concatenate([past_v, va.astype(past_v.dtype)], axis=1)
        oa = stick_breaking(qa, k_all, v_all, pos_s, b_sb[i])
        ob, s_s = retention(qb, kb, vb, state_ret[i])
        yb = retention_output(ob, gsw, gn_g[i], gn_b[i])
        xs = finish_layer(xs, oa, yb, ga, gb, p_sample[i], *lw)
        ks_l.append(ka)
        vs_l.append(va)
        ss_l.append(s_s)
    return (xp, xs, jnp.stack(kp_l), jnp.stack(vp_l), jnp.stack(ks_l), jnp.stack(vs_l), jnp.stack(sp_l), jnp.stack(ss_l))
```

```python
import functools

import jax
import jax.numpy as jnp
from jax import lax
from jax.experimental import pallas as pl
from jax.experimental.pallas import tpu as pltpu

F32 = jnp.float32
BF16 = jnp.bfloat16

D_MODEL = 4096
DEPTH = 4
PAGE_SIZE = 128
SB_HEADS = 16
SB_HEAD_DIM = 128
SB_WIDTH = SB_HEADS * SB_HEAD_DIM
RET_HEADS = 8
RET_HEAD_DIM = 256
RET_WIDTH = RET_HEADS * RET_HEAD_DIM
D_FF = 11008
PLE_DIM = 256
ALPHA = (2 * DEPTH) ** 0.25
LN_EPS = 1e-5
GN_EPS = 1e-6
ROPE_BASE = 10000.0

OFF_QA, OFF_KA, OFF_QB, OFF_KB, OFF_VB, OFF_GSW, OFF_G = 0, 2048, 6144, 8192, 10240, 12288, 14336

V7X_VMEM_LIMIT_BYTES = 56 * 1024 * 1024
SB_BLOCK = 256
RET_BLOCK = 256
DEC_ROWS_PER_HEAD = 8


def _params(*sem):
    return pltpu.CompilerParams(dimension_semantics=sem, vmem_limit_bytes=V7X_VMEM_LIMIT_BYTES)


def _dot(a, b):
    return jnp.dot(a, b, preferred_element_type=F32)


def _dot_nt(a, b):
    return lax.dot_general(a, b, (((1,), (1,)), ((), ())), preferred_element_type=F32)


def _dot_tn(a, b):
    return lax.dot_general(a, b, (((0,), (0,)), ((), ())), preferred_element_type=F32)


def _row_tile(m):
    return min(m, 1024)


def _proj_kernel(x_ref, w_ref, o_ref):
    o_ref[...] = _dot(x_ref[...], w_ref[...]).astype(o_ref.dtype)


def _proj(x, w, col_off, n_cols, out_dtype, tn=1024):
    m, k = x.shape
    tm = _row_tile(m)
    off = col_off // tn
    return pl.pallas_call(
        _proj_kernel,
        grid=(n_cols // tn, m // tm),
        in_specs=[pl.BlockSpec((tm, k), lambda n, i: (i, 0)),
                  pl.BlockSpec((k, tn), lambda n, i: (0, n + off))],
        out_specs=pl.BlockSpec((tm, tn), lambda n, i: (i, n)),
        out_shape=jax.ShapeDtypeStruct((m, n_cols), out_dtype),
        compiler_params=_params("parallel", "parallel"),
    )(x, w)


def _rot_kernel(x_ref, w_ref, cos_ref, sin_ref, o_ref, *, scale):
    acc = _dot(x_ref[...], w_ref[...])
    c = cos_ref[...]
    s = sin_ref[...]
    half = RET_HEAD_DIM // 2
    for j in range(acc.shape[1] // RET_HEAD_DIM):
        lo = j * RET_HEAD_DIM
        x1 = acc[:, lo:lo + half]
        x2 = acc[:, lo + half:lo + RET_HEAD_DIM]
        o_ref[:, lo:lo + half] = ((x1 * c - x2 * s) * scale).astype(o_ref.dtype)
        o_ref[:, lo + half:lo + RET_HEAD_DIM] = ((x1 * s + x2 * c) * scale).astype(o_ref.dtype)


def _proj_rotary(x, w, col_off, cos, sin, scale, out_dtype, tn=1024):
    m, k = x.shape
    tm = _row_tile(m)
    off = col_off // tn
    nrep = cos.shape[0] // tm
    return pl.pallas_call(
        functools.partial(_rot_kernel, scale=scale),
        grid=(RET_WIDTH // tn, m // tm),
        in_specs=[pl.BlockSpec((tm, k), lambda n, i: (i, 0)),
                  pl.BlockSpec((k, tn), lambda n, i: (0, n + off)),
                  pl.BlockSpec((tm, RET_HEAD_DIM // 2), lambda n, i: (i % nrep, 0)),
                  pl.BlockSpec((tm, RET_HEAD_DIM // 2), lambda n, i: (i % nrep, 0))],
        out_specs=pl.BlockSpec((tm, tn), lambda n, i: (i, n)),
        out_shape=jax.ShapeDtypeStruct((m, RET_WIDTH), out_dtype),
        compiler_params=_params("parallel", "parallel"),
    )(x, w, cos, sin)


def _strict_upper_ones(n):
    row = lax.broadcasted_iota(jnp.int32, (n, n), 0)
    col = lax.broadcasted_iota(jnp.int32, (n, n), 1)
    return jnp.where(row > col, 1.0, 0.0).astype(BF16)


def _sb_block(q, kb, vb, bias, later, ones_after, mask):
    z = _dot_nt(q, kb) * (SB_HEAD_DIM ** -0.5) + bias
    t = jnp.log1p(jnp.exp(-jnp.abs(z)))
    log_keep = -(jnp.maximum(z, 0.0) + t)
    if mask is not None:
        log_keep = jnp.where(mask, log_keep, 0.0)
    hi = log_keep.astype(BF16)
    lo = (log_keep - hi.astype(F32)).astype(BF16)
    between = _dot(hi, ones_after) + _dot(lo, ones_after)
    a = jnp.exp((jnp.minimum(z, 0.0) - t) + between + later)
    if mask is not None:
        a = jnp.where(mask, a, 0.0)
    return _dot(a.astype(BF16), vb), jnp.sum(log_keep, axis=1, keepdims=True)


def _sb_prompt_kernel(bias_ref, q_ref, k_ref, v_ref, o_ref, kbf_ref, vbf_ref):
    qi = pl.program_id(2)

    @pl.when(qi == 0)
    def _():
        kbf_ref[...] = k_ref[...].astype(BF16)
        vbf_ref[...] = v_ref[...].astype(BF16)

    blk = SB_BLOCK
    q = q_ref[...]
    bias = bias_ref[0]
    ones_after = _strict_upper_ones(blk)
    row = lax.broadcasted_iota(jnp.int32, (blk, blk), 0)
    col = lax.broadcasted_iota(jnp.int32, (blk, blk), 1)

    def tile(kblk, later, mask):
        start = pl.multiple_of(kblk * blk, blk)
        return _sb_block(q, kbf_ref[pl.ds(start, blk), :], vbf_ref[pl.ds(start, blk), :],
                         bias, later, ones_after, mask)

    acc, later = tile(qi, jnp.zeros((blk, 1), F32), col < row)

    def body(j, carry):
        acc, later = carry
        o, s = tile(qi - 1 - j, later, None)
        return acc + o, later + s

    acc, _ = lax.fori_loop(0, qi, body, (acc, later))
    o_ref[...] = acc.astype(o_ref.dtype)


def _sb_prompt(q, kv, bias, batch, seq):
    blk = SB_BLOCK
    nq = seq // blk
    bias_b = jnp.broadcast_to(bias.astype(F32)[:, None, None], (SB_HEADS, 1, blk))
    return pl.pallas_call(
        _sb_prompt_kernel,
        grid=(batch, SB_HEADS, nq),
        in_specs=[pl.BlockSpec((1, 1, blk), lambda b, h, i: (h, 0, 0)),
                  pl.BlockSpec((blk, SB_HEAD_DIM), lambda b, h, i: (b * nq + i, h)),
                  pl.BlockSpec((seq, SB_HEAD_DIM), lambda b, h, i: (b, h)),
                  pl.BlockSpec((seq, SB_HEAD_DIM), lambda b, h, i: (b, SB_HEADS + h))],
        out_specs=pl.BlockSpec((blk, SB_HEAD_DIM), lambda b, h, i: (b * nq + i, h)),
        out_shape=jax.ShapeDtypeStruct((batch * seq, SB_WIDTH), BF16),
        scratch_shapes=[pltpu.VMEM((seq, SB_HEAD_DIM), BF16), pltpu.VMEM((seq, SB_HEAD_DIM), BF16)],
        compiler_params=_params("parallel", "parallel", "arbitrary"),
    )(bias_b, q, kv, kv)


def _sb_decode_kernel(pt_ref, q_ref, bias_ref, kn_ref, vn_ref, kp_ref, vp_ref, o_ref, acc_ref, later_ref):
    del pt_ref
    p = pl.program_id(1)
    rows = SB_HEADS * DEC_ROWS_PER_HEAD
    q = q_ref[0]
    bias = bias_ref[...]
    ones_after = _strict_upper_ones(PAGE_SIZE)

    def tile(k, v, mask):
        o, s = _sb_block(q, k.astype(BF16), v.astype(BF16), bias, later_ref[...], ones_after, mask)
        acc_ref[...] += o
        later_ref[...] += s

    @pl.when(p == 0)
    def _():
        acc_ref[...] = jnp.zeros_like(acc_ref)
        later_ref[...] = jnp.zeros_like(later_ref)
        row = lax.broadcasted_iota(jnp.int32, (rows, PAGE_SIZE), 0)
        col = lax.broadcasted_iota(jnp.int32, (rows, PAGE_SIZE), 1)
        tile(kn_ref[0], vn_ref[0], col < (row % DEC_ROWS_PER_HEAD))

    tile(kp_ref[0, 0], vp_ref[0, 0], None)

    @pl.when(p == pl.num_programs(1) - 1)
    def _():
        r = DEC_ROWS_PER_HEAD
        for h in range(SB_HEADS):
            cols = slice(h * SB_HEAD_DIM, (h + 1) * SB_HEAD_DIM)
            o_ref[0, :, cols] = acc_ref[h * r:(h + 1) * r, cols]


def _sb_decode(q, kv_new, cache_k, cache_v, page_table, bias, layer):
    bs, n_pages = page_table.shape
    ts = q.shape[0] // bs
    r = DEC_ROWS_PER_HEAD
    rows = SB_HEADS * r
    q4 = jnp.pad(q.reshape(bs, ts, SB_HEADS, SB_HEAD_DIM), ((0, 0), (0, r - ts), (0, 0), (0, 0)))
    q4 = q4.transpose(0, 2, 1, 3)
    eye = jnp.eye(SB_HEADS, dtype=jnp.bool_)[None, :, None, :, None]
    qbd = jnp.where(eye, q4[:, :, :, None, :], jnp.zeros((), q.dtype)).reshape(bs, rows, SB_WIDTH)
    pad_new = lambda a: jnp.pad(a.reshape(bs, ts, SB_WIDTH), ((0, 0), (0, PAGE_SIZE - ts), (0, 0)))
    k_new = pad_new(kv_new[:, :SB_WIDTH])
    v_new = pad_new(kv_new[:, SB_WIDTH:])
    bias_rows = jnp.broadcast_to(jnp.repeat(bias.astype(F32), r)[:, None], (rows, PAGE_SIZE))
    pt_flat = page_table.reshape(-1)

    def page_map(b, p, pt):
        return (layer, pt[b * n_pages + (n_pages - 1 - p)], 0, 0)

    grid_spec = pltpu.PrefetchScalarGridSpec(
        num_scalar_prefetch=1,
        grid=(bs, n_pages),
        in_specs=[pl.BlockSpec((1, rows, SB_WIDTH), lambda b, p, pt: (b, 0, 0)),
                  pl.BlockSpec((rows, PAGE_SIZE), lambda b, p, pt: (0, 0)),
                  pl.BlockSpec((1, PAGE_SIZE, SB_WIDTH), lambda b, p, pt: (b, 0, 0)),
                  pl.BlockSpec((1, PAGE_SIZE, SB_WIDTH), lambda b, p, pt: (b, 0, 0)),
                  pl.BlockSpec((1, 1, PAGE_SIZE, SB_WIDTH), page_map),
                  pl.BlockSpec((1, 1, PAGE_SIZE, SB_WIDTH), page_map)],
        out_specs=pl.BlockSpec((1, r, SB_WIDTH), lambda b, p, pt: (b, 0, 0)),
        scratch_shapes=[pltpu.VMEM((rows, SB_WIDTH), F32), pltpu.VMEM((rows, PAGE_SIZE), F32)],
    )
    o = pl.pallas_call(
        _sb_decode_kernel,
        grid_spec=grid_spec,
        out_shape=jax.ShapeDtypeStruct((bs, r, SB_WIDTH), F32),
        compiler_params=_params("parallel", "arbitrary"),
    )(pt_flat, qbd, bias_rows, k_new, v_new, cache_k, cache_v)
    return o[:, :ts].reshape(bs * ts, SB_WIDTH).astype(BF16)


def _ret_kernel(*refs, has_s0):
    if has_s0:
        q_ref, k_ref, v_ref, g_ref, gng_ref, gnb_ref, dm_ref, qd_ref, kd_ref, cd_ref, s0_ref, y_ref, so_ref, s_ref = refs
    else:
        q_ref, k_ref, v_ref, g_ref, gng_ref, gnb_ref, dm_ref, qd_ref, kd_ref, cd_ref, y_ref, so_ref, s_ref = refs
    c = pl.program_id(2)

    @pl.when(c == 0)
    def _():
        if has_s0:
            s_ref[...] = s0_ref[0, 0]
        else:
            s_ref[...] = jnp.zeros_like(s_ref)

    q = q_ref[...]
    k = k_ref[...]
    v = v_ref[...]
    s = s_ref[...]
    inner = _dot_nt(q, k.astype(BF16)) * dm_ref[0]
    o = _dot(inner.astype(BF16), v) + _dot(q, s.astype(BF16)) * qd_ref[0]
    s_new = cd_ref[0] * s + _dot_tn((k * kd_ref[0]).astype(BF16), v)
    s_ref[...] = s_new

    mu = jnp.mean(o, axis=-1, keepdims=True)
    d = o - mu
    var = jnp.mean(d * d, axis=-1, keepdims=True)
    y = d * lax.rsqrt(var + GN_EPS) * gng_ref[...] + gnb_ref[...]
    g = g_ref[...]
    y_ref[...] = (g * jax.nn.sigmoid(g) * y).astype(y_ref.dtype)

    @pl.when(c == pl.num_programs(2) - 1)
    def _():
        so_ref[0, 0] = s_new


def _retention_tables(c_real, c_pad):
    lg = jnp.log1p(-jnp.exp2(-5.0 - jnp.arange(RET_HEADS, dtype=F32)))
    idx = jnp.arange(c_pad, dtype=F32)
    real = idx < c_real
    diff = idx[:, None] - idx[None, :]
    dmask = jnp.where(diff >= 0, jnp.exp(lg[:, None, None] * jnp.maximum(diff, 0.0)), 0.0)
    dmask = jnp.where(real[None, :, None] & real[None, None, :], dmask, 0.0)
    qdec = jnp.exp(lg[:, None] * (idx[None, :] + 1.0))
    kdec = jnp.where(real[None, :], jnp.exp(lg[:, None] * (c_real - 1.0 - idx[None, :])), 0.0)
    cdec = jnp.exp(lg * c_real)
    wide = lambda a: jnp.broadcast_to(a[:, :, None], (RET_HEADS, c_pad, RET_HEAD_DIM))
    return dmask, wide(qdec), wide(kdec), jnp.broadcast_to(cdec[:, None, None], (RET_HEADS, 1, RET_HEAD_DIM))


def _retention(q, k, v, gsw, gn_g, gn_b, s0, batch, n_chunks, c_real, c_pad):
    dmask, qdec, kdec, cdec = _retention_tables(c_real, c_pad)
    d = RET_HEAD_DIM
    row_spec = pl.BlockSpec((c_pad, d), lambda b, h, c: (b * n_chunks + c, h))
    head_vec = pl.BlockSpec((1, d), lambda b, h, c: (0, h))
    table = lambda r: pl.BlockSpec((1, r, d), lambda b, h, c: (h, 0, 0))
    state_spec = pl.BlockSpec((1, 1, d, d), lambda b, h, c: (b, h, 0, 0))
    in_specs = [row_spec, row_spec, row_spec, row_spec, head_vec, head_vec,
                pl.BlockSpec((1, c_pad, c_pad), lambda b, h, c: (h, 0, 0)), table(c_pad), table(c_pad), table(1)]
    args = [q, k, v, gsw, gn_g.reshape(1, -1).astype(F32), gn_b.reshape(1, -1).astype(F32), dmask, qdec, kdec, cdec]
    if s0 is not None:
        in_specs.append(state_spec)
        args.append(s0)
    return pl.pallas_call(
        functools.partial(_ret_kernel, has_s0=s0 is not None),
        grid=(batch, RET_HEADS, n_chunks),
        in_specs=in_specs,
        out_specs=[row_spec, state_spec],
        out_shape=[jax.ShapeDtypeStruct(q.shape, BF16),
                   jax.ShapeDtypeStruct((batch, RET_HEADS, d, d), F32)],
        scratch_shapes=[pltpu.VMEM((d, d), F32)],
        compiler_params=_params("parallel", "parallel", "arbitrary"),
    )(*args)


def _merge_kernel(oa_ref, yb_ref, wa_ref, wb_ref, ga_ref, gb_ref, o_ref):
    a = _dot(oa_ref[...], wa_ref[...])
    b = _dot(yb_ref[...], wb_ref[...])
    o_ref[...] = (jax.nn.sigmoid(ga_ref[...]) * a + jax.nn.sigmoid(gb_ref[...]) * b).astype(o_ref.dtype)


def _merge(oa, yb, wa, wb, gates, tn=512):
    m, k = oa.shape
    tm = _row_tile(m)
    gb_off = D_MODEL // tn
    return pl.pallas_call(
        _merge_kernel,
        grid=(D_MODEL // tn, m // tm),
        in_specs=[pl.BlockSpec((tm, k), lambda n, i: (i, 0)),
                  pl.BlockSpec((tm, k), lambda n, i: (i, 0)),
                  pl.BlockSpec((k, tn), lambda n, i: (0, n)),
                  pl.BlockSpec((k, tn), lambda n, i: (0, n)),
                  pl.BlockSpec((tm, tn), lambda n, i: (i, n)),
                  pl.BlockSpec((tm, tn), lambda n, i: (i, n + gb_off))],
        out_specs=pl.BlockSpec((tm, tn), lambda n, i: (i, n)),
        out_shape=jax.ShapeDtypeStruct((m, D_MODEL), BF16),
        compiler_params=_params("parallel", "parallel"),
    )(oa, yb, wa, wb, gates, gates)


def _resid_kernel(a_ref, w_ref, x_ref, o_ref):
    o_ref[...] = ALPHA * x_ref[...] + _dot(a_ref[...], w_ref[...])


def _resid_proj(a, w, x, tn=512):
    m, k = a.shape
    tm = _row_tile(m)
    return pl.pallas_call(
        _resid_kernel,
        grid=(D_MODEL // tn, m // tm),
        in_specs=[pl.BlockSpec((tm, k), lambda n, i: (i, 0)),
                  pl.BlockSpec((k, tn), lambda n, i: (0, n)),
                  pl.BlockSpec((tm, tn), lambda n, i: (i, n))],
        out_specs=pl.BlockSpec((tm, tn), lambda n, i: (i, n)),
        out_shape=jax.ShapeDtypeStruct((m, D_MODEL), F32),
        compiler_params=_params("parallel", "parallel"),
    )(a, w, x)


def _ln_kernel(x_ref, g_ref, b_ref, o_ref, obf_ref):
    x = x_ref[...]
    mu = jnp.mean(x, axis=-1, keepdims=True)
    d = x - mu
    var = jnp.mean(d * d, axis=-1, keepdims=True)
    y = d * lax.rsqrt(var + LN_EPS) * g_ref[...] + b_ref[...]
    o_ref[...] = y
    obf_ref[...] = y.astype(BF16)


def _layer_norm(x, g, b):
    m, n = x.shape
    tm = min(m, 256)
    row = pl.BlockSpec((tm, n), lambda i: (i, 0))
    vec = pl.BlockSpec((1, n), lambda i: (0, 0))
    return pl.pallas_call(
        _ln_kernel,
        grid=(m // tm,),
        in_specs=[row, vec, vec],
        out_specs=[row, row],
        out_shape=[jax.ShapeDtypeStruct((m, n), F32), jax.ShapeDtypeStruct((m, n), BF16)],
        compiler_params=_params("parallel"),
    )(x, g.reshape(1, n).astype(F32), b.reshape(1, n).astype(F32))


def _swiglu_kernel(x_ref, wg_ref, wu_ref, o_ref):
    x = x_ref[...]
    g = _dot(x, wg_ref[...])
    u = _dot(x, wu_ref[...])
    o_ref[...] = (g * jax.nn.sigmoid(g) * u).astype(o_ref.dtype)


def _swiglu(x, w, tn=256):
    m, k = x.shape
    tm = _row_tile(m)
    up_off = D_FF // tn
    return pl.pallas_call(
        _swiglu_kernel,
        grid=(D_FF // tn, m // tm),
        in_specs=[pl.BlockSpec((tm, k), lambda n, i: (i, 0)),
                  pl.BlockSpec((k, tn), lambda n, i: (0, n)),
                  pl.BlockSpec((k, tn), lambda n, i: (0, n + up_off))],
        out_specs=pl.BlockSpec((tm, tn), lambda n, i: (i, n)),
        out_shape=jax.ShapeDtypeStruct((m, D_FF), BF16),
        compiler_params=_params("parallel", "parallel"),
    )(x, w, w)


def _ple_kernel(xbf_ref, wg_ref, p_ref, wp_ref, x_ref, o_ref):
    gate = jax.nn.sigmoid(_dot(xbf_ref[...], wg_ref[...]))
    o_ref[...] = ALPHA * x_ref[...] + gate * _dot(p_ref[...], wp_ref[...])


def _ple(xbf, wg, p, wp, x, tn=512):
    m, k = xbf.shape
    tm = _row_tile(m)
    return pl.pallas_call(
        _ple_kernel,
        grid=(D_MODEL // tn, m // tm),
        in_specs=[pl.BlockSpec((tm, k), lambda n, i: (i, 0)),
                  pl.BlockSpec((k, tn), lambda n, i: (0, n)),
                  pl.BlockSpec((tm, PLE_DIM), lambda n, i: (i, 0)),
                  pl.BlockSpec((PLE_DIM, tn), lambda n, i: (0, n)),
                  pl.BlockSpec((tm, tn), lambda n, i: (i, n))],
        out_specs=pl.BlockSpec((tm, tn), lambda n, i: (i, n)),
        out_shape=jax.ShapeDtypeStruct((m, D_MODEL), F32),
        compiler_params=_params("parallel", "parallel"),
    )(xbf, wg, p, wp, x)


def _ffn_out_kernel(a_ref, w_ref, r_ref, o_ref):
    k = pl.program_id(2)
    d = _dot(a_ref[...], w_ref[...])

    @pl.when(k == 0)
    def _():
        o_ref[...] = r_ref[...] + d

    @pl.when(k > 0)
    def _():
        o_ref[...] += d


def _ffn_out(a, w, r, tn=512, nk=2):
    m, k = a.shape
    tm = _row_tile(m)
    tk = k // nk
    return pl.pallas_call(
        _ffn_out_kernel,
        grid=(D_MODEL // tn, m // tm, nk),
        in_specs=[pl.BlockSpec((tm, tk), lambda n, i, j: (i, j)),
                  pl.BlockSpec((tk, tn), lambda n, i, j: (j, n)),
                  pl.BlockSpec((tm, tn), lambda n, i, j: (i, n))],
        out_specs=pl.BlockSpec((tm, tn), lambda n, i, j: (i, n)),
        out_shape=jax.ShapeDtypeStruct((m, D_MODEL), F32),
        compiler_params=_params("parallel", "parallel", "arbitrary"),
    )(a, w, r)


def _project(xbf, w_in, cos, sin):
    q_a = _proj(xbf, w_in, OFF_QA, SB_WIDTH, BF16)
    kv_a = _proj(xbf, w_in, OFF_KA, 2 * SB_WIDTH, F32)
    q_b = _proj_rotary(xbf, w_in, OFF_QB, cos, sin, 1.0, BF16)
    k_b = _proj_rotary(xbf, w_in, OFF_KB, cos, sin, RET_HEAD_DIM ** -0.5, F32)
    v_b = _proj(xbf, w_in, OFF_VB, RET_WIDTH, BF16)
    gsw = _proj(xbf, w_in, OFF_GSW, RET_WIDTH, F32)
    gates = _proj(xbf, w_in, OFF_G, 2 * D_MODEL, F32)
    return q_a, kv_a, q_b, k_b, v_b, gsw, gates


def _finish(x, oa, yb, gates, p, lw):
    merged = _merge(oa, yb, lw["w_proj_a"], lw["w_proj_b"], gates)
    x1, x1bf = _layer_norm(_resid_proj(merged, lw["w_out"], x), lw["ln1_g"], lw["ln1_b"])
    act = _swiglu(x1bf, lw["w_ffn_in"])
    side = _ple(x1bf, lw["w_ple_gate"], p, lw["w_ple_proj"], x1)
    return _layer_norm(_ffn_out(act, lw["w_ffn_out"], side), lw["ln2_g"], lw["ln2_b"])


def _rotary_tables(pos):
    theta = ROPE_BASE ** (-jnp.arange(0, RET_HEAD_DIM, 2, dtype=F32) / RET_HEAD_DIM)
    ang = pos.astype(F32)[:, None] * theta[None, :]
    return jnp.cos(ang), jnp.sin(ang)


def _pad_rows(a, batch, t, t_pad):
    return jnp.pad(a.reshape(batch, t, -1), ((0, 0), (0, t_pad - t), (0, 0))).reshape(batch * t_pad, -1)


def kernel(x_prompt, x_sample, cache_sb_k, cache_sb_v, state_ret, page_table, p_prompt, p_sample,
           w_in, b_sb, gn_g, gn_b, w_proj_a, w_proj_b, w_out, ln1_g, ln1_b,
           w_ffn_in, w_ffn_out, w_ple_gate, w_ple_proj, ln2_g, ln2_b):
    bp, tp, _ = x_prompt.shape
    bs, ts, _ = x_sample.shape
    n_pool = cache_sb_k.shape[1]
    past = page_table.shape[1] * PAGE_SIZE
    cache_k = cache_sb_k.reshape(DEPTH, n_pool, PAGE_SIZE, SB_WIDTH)
    cache_v = cache_sb_v.reshape(DEPTH, n_pool, PAGE_SIZE, SB_WIDTH)

    cos_p, sin_p = _rotary_tables(jnp.arange(tp))
    cos_s, sin_s = _rotary_tables(jnp.tile(past + jnp.arange(ts), bs))
    ts_pad = 16

    xp = x_prompt.reshape(bp * tp, D_MODEL)
    xs = x_sample.reshape(bs * ts, D_MODEL)
    xp_bf, xs_bf = xp.astype(BF16), xs.astype(BF16)
    outs = [[] for _ in range(6)]
    for i in range(DEPTH):
        lw = {
            "w_proj_a": w_proj_a[i].astype(BF16), "w_proj_b": w_proj_b[i].astype(BF16),
            "w_out": w_out[i].astype(BF16), "ln1_g": ln1_g[i], "ln1_b": ln1_b[i],
            "w_ffn_in": w_ffn_in[i].astype(BF16), "w_ffn_out": w_ffn_out[i].astype(BF16),
            "w_ple_gate": w_ple_gate[i].astype(BF16), "w_ple_proj": w_ple_proj[i].astype(BF16),
            "ln2_g": ln2_g[i], "ln2_b": ln2_b[i],
        }
        w_in_bf = w_in[i].astype(BF16)

        q_a, kv_a, q_b, k_b, v_b, gsw, gates = _project(xp_bf, w_in_bf, cos_p, sin_p)
        oa = _sb_prompt(q_a, kv_a, b_sb[i], bp, tp)
        yb, s_p = _retention(q_b, k_b, v_b, gsw, gn_g[i], gn_b[i], None, bp, tp // RET_BLOCK, RET_BLOCK, RET_BLOCK)
        xp, xp_bf = _finish(xp, oa, yb, gates, p_prompt[i].reshape(bp * tp, PLE_DIM).astype(BF16), lw)
        outs[0].append(kv_a[:, :SB_WIDTH])
        outs[1].append(kv_a[:, SB_WIDTH:])
        outs[4].append(s_p)

        q_a, kv_a, q_b, k_b, v_b, gsw, gates = _project(xs_bf, w_in_bf, cos_s, sin_s)
        oa = _sb_decode(q_a, kv_a, cache_k, cache_v, page_table, b_sb[i], i)
        pad = lambda a: _pad_rows(a, bs, ts, ts_pad)
        yb, s_s = _retention(pad(q_b), pad(k_b), pad(v_b), pad(gsw), gn_g[i], gn_b[i], state_ret[i],
                             bs, 1, ts, ts_pad)
        yb = yb.reshape(bs, ts_pad, RET_WIDTH)[:, :ts].reshape(bs * ts, RET_WIDTH)
        xs, xs_bf = _finish(xs, oa, yb, gates, p_sample[i].reshape(bs * ts, PLE_DIM).astype(BF16), lw)
        outs[2].append(kv_a[:, :SB_WIDTH])
        outs[3].append(kv_a[:, SB_WIDTH:])
        outs[5].append(s_s)

    heads = lambda lst, b, t: jnp.stack(lst).reshape(DEPTH, b, t, SB_HEADS, SB_HEAD_DIM)
    return (xp.reshape(bp, tp, D_MODEL), xs.reshape(bs, ts, D_MODEL),
            heads(outs[0], bp, tp), heads(outs[1], bp, tp), heads(outs[2], bs, ts), heads(outs[3], bs, ts),
            jnp.stack(outs[4]), jnp.stack(outs[5]))
```

```python
import functools

import jax
import jax.numpy as jnp
from jax import lax
from jax.experimental import pallas as pl
from jax.experimental.pallas import tpu as pltpu

F32 = jnp.float32
BF16 = jnp.bfloat16

D_MODEL = 4096
DEPTH = 4
PAGE_SIZE = 128
SB_HEADS = 16
SB_HEAD_DIM = 128
SB_WIDTH = SB_HEADS * SB_HEAD_DIM
RET_HEADS = 8
RET_HEAD_DIM = 256
RET_WIDTH = RET_HEADS * RET_HEAD_DIM
D_FF = 11008
PLE_DIM = 256
ALPHA = (2 * DEPTH) ** 0.25
LN_EPS = 1e-5
GN_EPS = 1e-6
ROPE_BASE = 10000.0

OFF_QA, OFF_KA, OFF_VA, OFF_QB, OFF_KB, OFF_VB, OFF_GSW, OFF_G = 0, 2048, 4096, 6144, 8192, 10240, 12288, 14336

V7X_VMEM_LIMIT_BYTES = 56 * 1024 * 1024
MXU_WIDTH = 256
SB_TILE = 512
SB_SUB = MXU_WIDTH
RET_BLOCK = MXU_WIDTH
DEC_ROWS_PER_HEAD = 8
DEC_PAGES_PER_STEP = SB_TILE // PAGE_SIZE


def _params(*sem):
    return pltpu.CompilerParams(dimension_semantics=sem, vmem_limit_bytes=V7X_VMEM_LIMIT_BYTES)


def _dot(a, b):
    return jnp.dot(a, b, preferred_element_type=F32)


def _dot_nt(a, b):
    return lax.dot_general(a, b, (((1,), (1,)), ((), ())), preferred_element_type=F32)


def _dot_tn(a, b):
    return lax.dot_general(a, b, (((0,), (0,)), ((), ())), preferred_element_type=F32)


def _row_tile(m, cap=1024):
    return min(m, cap)


def _proj_kernel(x_ref, w_ref, o_ref):
    o_ref[...] = _dot(x_ref[...], w_ref[...]).astype(o_ref.dtype)


def _proj(x, w, col_off, n_cols, out_dtype, tn=1024):
    m, k = x.shape
    tm = _row_tile(m)
    off = col_off // tn
    return pl.pallas_call(
        _proj_kernel,
        grid=(n_cols // tn, m // tm),
        in_specs=[pl.BlockSpec((tm, k), lambda n, i: (i, 0)),
                  pl.BlockSpec((k, tn), lambda n, i: (0, n + off))],
        out_specs=pl.BlockSpec((tm, tn), lambda n, i: (i, n)),
        out_shape=jax.ShapeDtypeStruct((m, n_cols), out_dtype),
        compiler_params=_params("parallel", "parallel"),
        name="proj",
    )(x, w)


def _proj_slot_kernel(x_ref, w_ref, buf_ref, o_ref):
    del buf_ref
    o_ref[0] = _dot(x_ref[...], w_ref[...])


def _proj_into_slot(buf, x, w, col_off, layer, tn=1024):
    m, k = x.shape
    tm = _row_tile(m)
    off = col_off // tn
    return pl.pallas_call(
        _proj_slot_kernel,
        grid=(buf.shape[2] // tn, m // tm),
        in_specs=[pl.BlockSpec((tm, k), lambda n, i: (i, 0)),
                  pl.BlockSpec((k, tn), lambda n, i: (0, n + off)),
                  pl.BlockSpec(memory_space=pl.ANY)],
        out_specs=pl.BlockSpec((1, tm, tn), lambda n, i: (layer, i, n)),
        out_shape=jax.ShapeDtypeStruct(buf.shape, buf.dtype),
        input_output_aliases={2: 0},
        compiler_params=_params("parallel", "parallel"),
        name="proj_slot",
    )(x, w, buf)


def _rot_kernel(x_ref, w_ref, cos_ref, sin_ref, o_ref, *, scale):
    acc = _dot(x_ref[...], w_ref[...])
    c = cos_ref[...]
    s = sin_ref[...]
    half = RET_HEAD_DIM // 2
    for j in range(acc.shape[1] // RET_HEAD_DIM):
        lo = j * RET_HEAD_DIM
        x1 = acc[:, lo:lo + half]
        x2 = acc[:, lo + half:lo + RET_HEAD_DIM]
        o_ref[:, lo:lo + half] = ((x1 * c - x2 * s) * scale).astype(o_ref.dtype)
        o_ref[:, lo + half:lo + RET_HEAD_DIM] = ((x1 * s + x2 * c) * scale).astype(o_ref.dtype)


def _proj_rotary(x, w, col_off, cos, sin, scale, out_dtype, tn=1024):
    m, k = x.shape
    tm = _row_tile(m)
    off = col_off // tn
    nrep = cos.shape[0] // tm
    return pl.pallas_call(
        functools.partial(_rot_kernel, scale=scale),
        grid=(RET_WIDTH // tn, m // tm),
        in_specs=[pl.BlockSpec((tm, k), lambda n, i: (i, 0)),
                  pl.BlockSpec((k, tn), lambda n, i: (0, n + off)),
                  pl.BlockSpec((tm, RET_HEAD_DIM // 2), lambda n, i: (i % nrep, 0)),
                  pl.BlockSpec((tm, RET_HEAD_DIM // 2), lambda n, i: (i % nrep, 0))],
        out_specs=pl.BlockSpec((tm, tn), lambda n, i: (i, n)),
        out_shape=jax.ShapeDtypeStruct((m, RET_WIDTH), out_dtype),
        compiler_params=_params("parallel", "parallel"),
        name="proj_rotary",
    )(x, w, cos, sin)


def _strict_upper_ones(n):
    row = lax.broadcasted_iota(jnp.int32, (n, n), 0)
    col = lax.broadcasted_iota(jnp.int32, (n, n), 1)
    return jnp.where(row > col, 1.0, 0.0).astype(BF16)


def _sb_tile(q, k, v, bias, later, mask, sub):
    z = _dot_nt(q, k) * (SB_HEAD_DIM ** -0.5) + bias
    t = jnp.log(1.0 + jnp.exp(-jnp.abs(z)))
    softplus = jnp.maximum(z, 0.0) + t
    if mask is not None:
        softplus = jnp.where(mask, softplus, 0.0)
    log_beta = jnp.minimum(z, 0.0) - t
    ones_after = _strict_upper_ones(sub)
    parts = []
    for j in reversed(range(k.shape[0] // sub)):
        cols = slice(j * sub, (j + 1) * sub)
        s = softplus[:, cols]
        hi = s.astype(BF16)
        lo = (s - hi.astype(F32)).astype(BF16)
        after = _dot(hi, ones_after) + _dot(lo, ones_after) + later
        parts.append(jnp.exp(log_beta[:, cols] - after))
        later = later + jnp.sum(s, axis=1, keepdims=True)
    a = parts[0] if len(parts) == 1 else jnp.concatenate(parts[::-1], axis=1)
    if mask is not None:
        a = jnp.where(mask, a, 0.0)
    return _dot(a.astype(BF16), v), later


def _sb_prompt_kernel(bias_ref, q_ref, k_ref, v_ref, o_ref, kbf_ref, vbf_ref, acc_ref):
    qi = pl.program_id(2)

    @pl.when(qi == 0)
    def _():
        kbf_ref[...] = k_ref[...].astype(BF16)
        vbf_ref[...] = v_ref[...].astype(BF16)

    tile = SB_TILE
    q = q_ref[...]
    bias = bias_ref[0]

    def step(kt, later, mask):
        start = pl.multiple_of(kt * tile, tile)
        o, later = _sb_tile(q, kbf_ref[pl.ds(start, tile), :], vbf_ref[pl.ds(start, tile), :],
                            bias, later, mask, SB_SUB)
        return o, later

    row = lax.broadcasted_iota(jnp.int32, (tile, tile), 0)
    col = lax.broadcasted_iota(jnp.int32, (tile, tile), 1)
    o, later = step(qi, jnp.zeros((tile, 1), F32), col < row)
    acc_ref[...] = o

    def body(j, later):
        o, later = step(qi - 1 - j, later, None)
        acc_ref[...] += o
        return later

    lax.fori_loop(0, qi, body, later)
    o_ref[...] = acc_ref[...].astype(o_ref.dtype)


def _sb_prompt(q, kbuf, vbuf, layer, bias, batch, seq):
    tile = SB_TILE
    nq = seq // tile
    bias_b = jnp.broadcast_to(bias.astype(F32)[:, None, None], (SB_HEADS, 1, tile))
    kv_spec = pl.BlockSpec((None, seq, SB_HEAD_DIM), lambda b, h, i: (layer, b, h))
    return pl.pallas_call(
        _sb_prompt_kernel,
        grid=(batch, SB_HEADS, nq),
        in_specs=[pl.BlockSpec((1, 1, tile), lambda b, h, i: (h, 0, 0)),
                  pl.BlockSpec((tile, SB_HEAD_DIM), lambda b, h, i: (b * nq + i, h)),
                  kv_spec, kv_spec],
        out_specs=pl.BlockSpec((tile, SB_HEAD_DIM), lambda b, h, i: (b * nq + i, h)),
        out_shape=jax.ShapeDtypeStruct((batch * seq, SB_WIDTH), BF16),
        scratch_shapes=[pltpu.VMEM((seq, SB_HEAD_DIM), BF16), pltpu.VMEM((seq, SB_HEAD_DIM), BF16),
                        pltpu.VMEM((tile, SB_HEAD_DIM), F32)],
        compiler_params=_params("parallel", "parallel", "arbitrary"),
        name="sb_prompt",
    )(bias_b, q, kbuf, vbuf)


def _heads_to_lanes(page_ref):
    return jnp.concatenate(
        [page_ref[0, 0, pl.ds(h, PAGE_SIZE, stride=SB_HEADS), :].astype(BF16) for h in range(SB_HEADS)], axis=1)


def _sb_decode_kernel(pt_ref, q_ref, bias_ref, kn_ref, vn_ref, *refs):
    del pt_ref
    n = DEC_PAGES_PER_STEP
    k_refs, v_refs = refs[:n], refs[n:2 * n]
    o_ref, acc_ref, later_ref = refs[2 * n:]
    p = pl.program_id(1)
    rows = SB_HEADS * DEC_ROWS_PER_HEAD
    q = q_ref[0]
    bias = bias_ref[...]

    @pl.when(p == 0)
    def _():
        row = lax.broadcasted_iota(jnp.int32, (rows, PAGE_SIZE), 0)
        col = lax.broadcasted_iota(jnp.int32, (rows, PAGE_SIZE), 1)
        o, later = _sb_tile(q, kn_ref[0].astype(BF16), vn_ref[0].astype(BF16), bias,
                            jnp.zeros((rows, 1), F32), col < (row % DEC_ROWS_PER_HEAD), PAGE_SIZE)
        acc_ref[...] = o
        later_ref[...] = later

    k = jnp.concatenate([_heads_to_lanes(r) for r in k_refs[::-1]], axis=0)
    v = jnp.concatenate([_heads_to_lanes(r) for r in v_refs[::-1]], axis=0)
    o, later = _sb_tile(q, k, v, bias, later_ref[...], None, SB_SUB)
    acc_ref[...] += o
    later_ref[...] = later

    @pl.when(p == pl.num_programs(1) - 1)
    def _():
        r = DEC_ROWS_PER_HEAD
        for h in range(SB_HEADS):
            cols = slice(h * SB_HEAD_DIM, (h + 1) * SB_HEAD_DIM)
            o_ref[0, :, cols] = acc_ref[h * r:(h + 1) * r, cols]


def _sb_decode(q, kv_new, cache_k, cache_v, page_table, bias, layer):
    bs, n_pages = page_table.shape
    ts = q.shape[0] // bs
    r = DEC_ROWS_PER_HEAD
    rows = SB_HEADS * r
    n = DEC_PAGES_PER_STEP
    q4 = jnp.pad(q.reshape(bs, ts, SB_HEADS, SB_HEAD_DIM), ((0, 0), (0, r - ts), (0, 0), (0, 0)))
    q4 = q4.transpose(0, 2, 1, 3)
    eye = jnp.eye(SB_HEADS, dtype=jnp.bool_)[None, :, None, :, None]
    qbd = jnp.where(eye, q4[:, :, :, None, :], jnp.zeros((), q.dtype)).reshape(bs, rows, SB_WIDTH)
    pad_new = lambda a: jnp.pad(a.reshape(bs, ts, SB_WIDTH), ((0, 0), (0, PAGE_SIZE - ts), (0, 0)))
    k_new = pad_new(kv_new[:, :SB_WIDTH])
    v_new = pad_new(kv_new[:, SB_WIDTH:])
    bias_rows = jnp.repeat(bias.astype(F32), r)[:, None]
    pt_flat = page_table.reshape(-1)

    def page_spec(slot):
        def page_map(b, p, pt):
            return (layer, pt[b * n_pages + (n_pages - 1 - (p * n + slot))], 0, 0)
        return pl.BlockSpec((1, 1, PAGE_SIZE * SB_HEADS, SB_HEAD_DIM), page_map)

    new_spec = pl.BlockSpec((1, PAGE_SIZE, SB_WIDTH), lambda b, p, pt: (b, 0, 0))
    grid_spec = pltpu.PrefetchScalarGridSpec(
        num_scalar_prefetch=1,
        grid=(bs, n_pages // n),
        in_specs=[pl.BlockSpec((1, rows, SB_WIDTH), lambda b, p, pt: (b, 0, 0)),
                  pl.BlockSpec((rows, 1), lambda b, p, pt: (0, 0)),
                  new_spec, new_spec] + [page_spec(s) for s in range(n)] * 2,
        out_specs=pl.BlockSpec((1, r, SB_WIDTH), lambda b, p, pt: (b, 0, 0)),
        scratch_shapes=[pltpu.VMEM((rows, SB_WIDTH), F32), pltpu.VMEM((rows, 1), F32)],
    )
    o = pl.pallas_call(
        _sb_decode_kernel,
        grid_spec=grid_spec,
        out_shape=jax.ShapeDtypeStruct((bs, r, SB_WIDTH), F32),
        compiler_params=_params("parallel", "arbitrary"),
        name="sb_decode",
    )(pt_flat, qbd, bias_rows, k_new, v_new, *([cache_k] * n), *([cache_v] * n))
    return o[:, :ts].reshape(bs * ts, SB_WIDTH).astype(BF16)


def _ret_kernel(*refs, has_s0):
    if has_s0:
        q_ref, k_ref, v_ref, g_ref, gng_ref, gnb_ref, dm_ref, qd_ref, kd_ref, cd_ref, s0_ref, y_ref, so_ref, s_ref = refs
    else:
        q_ref, k_ref, v_ref, g_ref, gng_ref, gnb_ref, dm_ref, qd_ref, kd_ref, cd_ref, y_ref, so_ref, s_ref = refs
    c = pl.program_id(2)

    @pl.when(c == 0)
    def _():
        if has_s0:
            s_ref[...] = s0_ref[0, 0]
        else:
            s_ref[...] = jnp.zeros_like(s_ref)

    q = q_ref[...]
    k = k_ref[...]
    v = v_ref[...]
    s = s_ref[...]
    inner = _dot_nt(q, k.astype(BF16)) * dm_ref[0]
    o = _dot(inner.astype(BF16), v) + _dot(q, s.astype(BF16)) * qd_ref[0]
    s_new = cd_ref[0] * s + _dot_tn((k * kd_ref[0]).astype(BF16), v)
    s_ref[...] = s_new

    mu = jnp.mean(o, axis=-1, keepdims=True)
    d = o - mu
    var = jnp.mean(d * d, axis=-1, keepdims=True)
    y = d * lax.rsqrt(var + GN_EPS) * gng_ref[...] + gnb_ref[...]
    g = g_ref[...]
    y_ref[...] = (g * jax.nn.sigmoid(g) * y).astype(y_ref.dtype)

    @pl.when(c == pl.num_programs(2) - 1)
    def _():
        so_ref[0, 0] = s_new


def _retention_tables(c_real, c_pad):
    lg = jnp.log1p(-jnp.exp2(-5.0 - jnp.arange(RET_HEADS, dtype=F32)))
    idx = jnp.arange(c_pad, dtype=F32)
    real = idx < c_real
    diff = idx[:, None] - idx[None, :]
    dmask = jnp.where(diff >= 0, jnp.exp(lg[:, None, None] * jnp.maximum(diff, 0.0)), 0.0)
    dmask = jnp.where(real[None, :, None] & real[None, None, :], dmask, 0.0)
    qdec = jnp.exp(lg[:, None] * (idx[None, :] + 1.0))
    kdec = jnp.where(real[None, :], jnp.exp(lg[:, None] * (c_real - 1.0 - idx[None, :])), 0.0)
    cdec = jnp.exp(lg * c_real)
    wide = lambda a: jnp.broadcast_to(a[:, :, None], (RET_HEADS, c_pad, RET_HEAD_DIM))
    return dmask, wide(qdec), wide(kdec), jnp.broadcast_to(cdec[:, None, None], (RET_HEADS, 1, RET_HEAD_DIM))


def _retention(q, k, v, gsw, gn_g, gn_b, s0, batch, n_chunks, c_real, c_pad):
    dmask, qdec, kdec, cdec = _retention_tables(c_real, c_pad)
    d = RET_HEAD_DIM
    row_spec = pl.BlockSpec((c_pad, d), lambda b, h, c: (b * n_chunks + c, h))
    head_vec = pl.BlockSpec((1, d), lambda b, h, c: (0, h))
    table = lambda r: pl.BlockSpec((1, r, d), lambda b, h, c: (h, 0, 0))
    state_spec = pl.BlockSpec((1, 1, d, d), lambda b, h, c: (b, h, 0, 0))
    in_specs = [row_spec, row_spec, row_spec, row_spec, head_vec, head_vec,
                pl.BlockSpec((1, c_pad, c_pad), lambda b, h, c: (h, 0, 0)), table(c_pad), table(c_pad), table(1)]
    args = [q, k, v, gsw, gn_g.reshape(1, -1).astype(F32), gn_b.reshape(1, -1).astype(F32), dmask, qdec, kdec, cdec]
    if s0 is not None:
        in_specs.append(state_spec)
        args.append(s0)
    return pl.pallas_call(
        functools.partial(_ret_kernel, has_s0=s0 is not None),
        grid=(batch, RET_HEADS, n_chunks),
        in_specs=in_specs,
        out_specs=[row_spec, state_spec],
        out_shape=[jax.ShapeDtypeStruct(q.shape, BF16),
                   jax.ShapeDtypeStruct((batch, RET_HEADS, d, d), F32)],
        scratch_shapes=[pltpu.VMEM((d, d), F32)],
        compiler_params=_params("parallel", "parallel", "arbitrary"),
        name="retention",
    )(*args)


def _merge_kernel(oa_ref, yb_ref, wa_ref, wb_ref, ga_ref, gb_ref, o_ref):
    a = _dot(oa_ref[...], wa_ref[...])
    b = _dot(yb_ref[...], wb_ref[...])
    o_ref[...] = (jax.nn.sigmoid(ga_ref[...]) * a + jax.nn.sigmoid(gb_ref[...]) * b).astype(o_ref.dtype)


def _merge(oa, yb, wa, wb, gates, tn=512):
    m, k = oa.shape
    tm = _row_tile(m)
    gb_off = D_MODEL // tn
    return pl.pallas_call(
        _merge_kernel,
        grid=(D_MODEL // tn, m // tm),
        in_specs=[pl.BlockSpec((tm, k), lambda n, i: (i, 0)),
                  pl.BlockSpec((tm, k), lambda n, i: (i, 0)),
                  pl.BlockSpec((k, tn), lambda n, i: (0, n)),
                  pl.BlockSpec((k, tn), lambda n, i: (0, n)),
                  pl.BlockSpec((tm, tn), lambda n, i: (i, n)),
                  pl.BlockSpec((tm, tn), lambda n, i: (i, n + gb_off))],
        out_specs=pl.BlockSpec((tm, tn), lambda n, i: (i, n)),
        out_shape=jax.ShapeDtypeStruct((m, D_MODEL), BF16),
        compiler_params=_params("parallel", "parallel"),
        name="merge",
    )(oa, yb, wa, wb, gates, gates)


def _resid_kernel(a_ref, w_ref, x_ref, o_ref):
    o_ref[...] = ALPHA * x_ref[...] + _dot(a_ref[...], w_ref[...])


def _resid_proj(a, w, x, tn=512):
    m, k = a.shape
    tm = _row_tile(m)
    return pl.pallas_call(
        _resid_kernel,
        grid=(D_MODEL // tn, m // tm),
        in_specs=[pl.BlockSpec((tm, k), lambda n, i: (i, 0)),
                  pl.BlockSpec((k, tn), lambda n, i: (0, n)),
                  pl.BlockSpec((tm, tn), lambda n, i: (i, n))],
        out_specs=pl.BlockSpec((tm, tn), lambda n, i: (i, n)),
        out_shape=jax.ShapeDtypeStruct((m, D_MODEL), F32),
        compiler_params=_params("parallel", "parallel"),
        name="resid_proj",
    )(a, w, x)


def _ln_kernel(x_ref, g_ref, b_ref, o_ref, obf_ref):
    x = x_ref[...]
    mu = jnp.mean(x, axis=-1, keepdims=True)
    d = x - mu
    var = jnp.mean(d * d, axis=-1, keepdims=True)
    y = d * lax.rsqrt(var + LN_EPS) * g_ref[...] + b_ref[...]
    o_ref[...] = y
    obf_ref[...] = y.astype(BF16)


def _layer_norm(x, g, b):
    m, n = x.shape
    tm = min(m, 256)
    row = pl.BlockSpec((tm, n), lambda i: (i, 0))
    vec = pl.BlockSpec((1, n), lambda i: (0, 0))
    return pl.pallas_call(
        _ln_kernel,
        grid=(m // tm,),
        in_specs=[row, vec, vec],
        out_specs=[row, row],
        out_shape=[jax.ShapeDtypeStruct((m, n), F32), jax.ShapeDtypeStruct((m, n), BF16)],
        compiler_params=_params("parallel"),
        name="layer_norm",
    )(x, g.reshape(1, n).astype(F32), b.reshape(1, n).astype(F32))


def _swiglu_kernel(x_ref, wg_ref, wu_ref, o_ref):
    x = x_ref[...]
    g = _dot(x, wg_ref[...])
    u = _dot(x, wu_ref[...])
    o_ref[...] = (g * jax.nn.sigmoid(g) * u).astype(o_ref.dtype)


def _swiglu(x, w, tn=256):
    m, k = x.shape
    tm = _row_tile(m, 2048)
    up_off = D_FF // tn
    return pl.pallas_call(
        _swiglu_kernel,
        grid=(D_FF // tn, m // tm),
        in_specs=[pl.BlockSpec((tm, k), lambda n, i: (i, 0)),
                  pl.BlockSpec((k, tn), lambda n, i: (0, n)),
                  pl.BlockSpec((k, tn), lambda n, i: (0, n + up_off))],
        out_specs=pl.BlockSpec((tm, tn), lambda n, i: (i, n)),
        out_shape=jax.ShapeDtypeStruct((m, D_FF), BF16),
        compiler_params=_params("parallel", "parallel"),
        name="swiglu",
    )(x, w, w)


def _ple_kernel(xbf_ref, wg_ref, p_ref, wp_ref, x_ref, o_ref):
    gate = jax.nn.sigmoid(_dot(xbf_ref[...], wg_ref[...]))
    o_ref[...] = ALPHA * x_ref[...] + gate * _dot(p_ref[...], wp_ref[...])


def _ple(xbf, wg, p, wp, x, tn=512):
    m, k = xbf.shape
    tm = _row_tile(m)
    return pl.pallas_call(
        _ple_kernel,
        grid=(D_MODEL // tn, m // tm),
        in_specs=[pl.BlockSpec((tm, k), lambda n, i: (i, 0)),
                  pl.BlockSpec((k, tn), lambda n, i: (0, n)),
                  pl.BlockSpec((tm, PLE_DIM), lambda n, i: (i, 0)),
                  pl.BlockSpec((PLE_DIM, tn), lambda n, i: (0, n)),
                  pl.BlockSpec((tm, tn), lambda n, i: (i, n))],
        out_specs=pl.BlockSpec((tm, tn), lambda n, i: (i, n)),
        out_shape=jax.ShapeDtypeStruct((m, D_MODEL), F32),
        compiler_params=_params("parallel", "parallel"),
        name="ple",
    )(xbf, wg, p, wp, x)


def _ffn_out_kernel(a_ref, w_ref, r_ref, o_ref):
    k = pl.program_id(2)
    d = _dot(a_ref[...], w_ref[...])

    @pl.when(k == 0)
    def _():
        o_ref[...] = r_ref[...] + d

    @pl.when(k > 0)
    def _():
        o_ref[...] += d


def _ffn_out(a, w, r, tn=512, nk=2):
    m, k = a.shape
    tm = _row_tile(m)
    tk = k // nk
    return pl.pallas_call(
        _ffn_out_kernel,
        grid=(D_MODEL // tn, m // tm, nk),
        in_specs=[pl.BlockSpec((tm, tk), lambda n, i, j: (i, j)),
                  pl.BlockSpec((tk, tn), lambda n, i, j: (j, n)),
                  pl.BlockSpec((tm, tn), lambda n, i, j: (i, n))],
        out_specs=pl.BlockSpec((tm, tn), lambda n, i, j: (i, n)),
        out_shape=jax.ShapeDtypeStruct((m, D_MODEL), F32),
        compiler_params=_params("parallel", "parallel", "arbitrary"),
        name="ffn_out",
    )(a, w, r)


def _project_branches(xbf, w_in, cos, sin):
    q_a = _proj(xbf, w_in, OFF_QA, SB_WIDTH, BF16)
    q_b = _proj_rotary(xbf, w_in, OFF_QB, cos, sin, 1.0, BF16)
    k_b = _proj_rotary(xbf, w_in, OFF_KB, cos, sin, RET_HEAD_DIM ** -0.5, F32)
    v_b = _proj(xbf, w_in, OFF_VB, RET_WIDTH, BF16)
    gsw = _proj(xbf, w_in, OFF_GSW, RET_WIDTH, F32)
    gates = _proj(xbf, w_in, OFF_G, 2 * D_MODEL, F32)
    return q_a, q_b, k_b, v_b, gsw, gates


def _finish(x, oa, yb, gates, p, lw):
    merged = _merge(oa, yb, lw["w_proj_a"], lw["w_proj_b"], gates)
    x1, x1bf = _layer_norm(_resid_proj(merged, lw["w_out"], x), lw["ln1_g"], lw["ln1_b"])
    act = _swiglu(x1bf, lw["w_ffn_in"])
    side = _ple(x1bf, lw["w_ple_gate"], p, lw["w_ple_proj"], x1)
    return _layer_norm(_ffn_out(act, lw["w_ffn_out"], side), lw["ln2_g"], lw["ln2_b"])


def _rotary_tables(pos):
    theta = ROPE_BASE ** (-jnp.arange(0, RET_HEAD_DIM, 2, dtype=F32) / RET_HEAD_DIM)
    ang = pos.astype(F32)[:, None] * theta[None, :]
    return jnp.cos(ang), jnp.sin(ang)


def _pad_rows(a, batch, t, t_pad):
    return jnp.pad(a.reshape(batch, t, -1), ((0, 0), (0, t_pad - t), (0, 0))).reshape(batch * t_pad, -1)


def kernel(x_prompt, x_sample, cache_sb_k, cache_sb_v, state_ret, page_table, p_prompt, p_sample,
           w_in, b_sb, gn_g, gn_b, w_proj_a, w_proj_b, w_out, ln1_g, ln1_b,
           w_ffn_in, w_ffn_out, w_ple_gate, w_ple_proj, ln2_g, ln2_b):
    bp, tp, _ = x_prompt.shape
    bs, ts, _ = x_sample.shape
    n_pool = cache_sb_k.shape[1]
    past = page_table.shape[1] * PAGE_SIZE
    cache_k = cache_sb_k.reshape(DEPTH, n_pool, PAGE_SIZE * SB_HEADS, SB_HEAD_DIM)
    cache_v = cache_sb_v.reshape(DEPTH, n_pool, PAGE_SIZE * SB_HEADS, SB_HEAD_DIM)

    cos_p, sin_p = _rotary_tables(jnp.arange(tp))
    cos_s, sin_s = _rotary_tables(jnp.tile(past + jnp.arange(ts), bs))
    ts_pad = 16

    xp = x_prompt.reshape(bp * tp, D_MODEL)
    xs = x_sample.reshape(bs * ts, D_MODEL)
    xp_bf, xs_bf = xp.astype(BF16), xs.astype(BF16)
    kbuf = jnp.zeros((DEPTH, bp * tp, SB_WIDTH), F32)
    vbuf = jnp.zeros((DEPTH, bp * tp, SB_WIDTH), F32)
    ks_l, vs_l, sp_l, ss_l = [], [], [], []
    for i in range(DEPTH):
        lw = {
            "w_proj_a": w_proj_a[i].astype(BF16), "w_proj_b": w_proj_b[i].astype(BF16),
            "w_out": w_out[i].astype(BF16), "ln1_g": ln1_g[i], "ln1_b": ln1_b[i],
            "w_ffn_in": w_ffn_in[i].astype(BF16), "w_ffn_out": w_ffn_out[i].astype(BF16),
            "w_ple_gate": w_ple_gate[i].astype(BF16), "w_ple_proj": w_ple_proj[i].astype(BF16),
            "ln2_g": ln2_g[i], "ln2_b": ln2_b[i],
        }
        w_in_bf = w_in[i].astype(BF16)

        kbuf = _proj_into_slot(kbuf, xp_bf, w_in_bf, OFF_KA, i)
        vbuf = _proj_into_slot(vbuf, xp_bf, w_in_bf, OFF_VA, i)
        q_a, q_b, k_b, v_b, gsw, gates = _project_branches(xp_bf, w_in_bf, cos_p, sin_p)
        oa = _sb_prompt(q_a, kbuf, vbuf, i, b_sb[i], bp, tp)
        yb, s_p = _retention(q_b, k_b, v_b, gsw, gn_g[i], gn_b[i], None, bp, tp // RET_BLOCK, RET_BLOCK, RET_BLOCK)
        xp, xp_bf = _finish(xp, oa, yb, gates, p_prompt[i].reshape(bp * tp, PLE_DIM).astype(BF16), lw)
        sp_l.append(s_p)

        kv_a = _proj(xs_bf, w_in_bf, OFF_KA, 2 * SB_WIDTH, F32)
        q_a, q_b, k_b, v_b, gsw, gates = _project_branches(xs_bf, w_in_bf, cos_s, sin_s)
        oa = _sb_decode(q_a, kv_a, cache_k, cache_v, page_table, b_sb[i], i)
        pad = lambda a: _pad_rows(a, bs, ts, ts_pad)
        yb, s_s = _retention(pad(q_b), pad(k_b), pad(v_b), pad(gsw), gn_g[i], gn_b[i], state_ret[i],
                             bs, 1, ts, ts_pad)
        yb = yb.reshape(bs, ts_pad, RET_WIDTH)[:, :ts].reshape(bs * ts, RET_WIDTH)
        xs, xs_bf = _finish(xs, oa, yb, gates, p_sample[i].reshape(bs * ts, PLE_DIM).astype(BF16), lw)
        ks_l.append(kv_a[:, :SB_WIDTH])
        vs_l.append(kv_a[:, SB_WIDTH:])
        ss_l.append(s_s)

    heads = lambda a, b, t: a.reshape(DEPTH, b, t, SB_HEADS, SB_HEAD_DIM)
    return (xp.reshape(bp, tp, D_MODEL), xs.reshape(bs, ts, D_MODEL),
            heads(kbuf, bp, tp), heads(vbuf, bp, tp),
            heads(jnp.stack(ks_l), bs, ts), heads(jnp.stack(vs_l), bs, ts),
            jnp.stack(sp_l), jnp.stack(ss_l))
```

```python
import functools

import jax
import jax.numpy as jnp
from jax import lax
from jax.experimental import pallas as pl
from jax.experimental.pallas import tpu as pltpu

F32 = jnp.float32
BF16 = jnp.bfloat16

D_MODEL = 4096
DEPTH = 4
PAGE_SIZE = 128
SB_HEADS = 16
SB_HEAD_DIM = 128
SB_WIDTH = SB_HEADS * SB_HEAD_DIM
RET_HEADS = 8
RET_HEAD_DIM = 256
RET_WIDTH = RET_HEADS * RET_HEAD_DIM
D_FF = 11008
PLE_DIM = 256
ALPHA = (2 * DEPTH) ** 0.25
LN_EPS = 1e-5
GN_EPS = 1e-6
ROPE_BASE = 10000.0

OFF_QA, OFF_KA, OFF_VA, OFF_QB, OFF_KB, OFF_VB, OFF_GSW, OFF_G = 0, 2048, 4096, 6144, 8192, 10240, 12288, 14336

V7X_VMEM_LIMIT_BYTES = 56 * 1024 * 1024
MXU_WIDTH = 256
ROW_TILE = 1024
COL_TILE = 512
SB_TILE = 512
SB_SUB = MXU_WIDTH
RET_BLOCK = MXU_WIDTH
DEC_ROWS_PER_HEAD = 8
DEC_PAGES_PER_STEP = SB_TILE // PAGE_SIZE


def _params(*sem):
    return pltpu.CompilerParams(dimension_semantics=sem, vmem_limit_bytes=V7X_VMEM_LIMIT_BYTES)


def _dot(a, b):
    return jnp.dot(a, b, preferred_element_type=F32)


def _dot_nt(a, b):
    return lax.dot_general(a, b, (((1,), (1,)), ((), ())), preferred_element_type=F32)


def _dot_tn(a, b):
    return lax.dot_general(a, b, (((0,), (0,)), ((), ())), preferred_element_type=F32)


def _bf(w_ref):
    return w_ref[...].astype(BF16)


def _row_steps(m, tm):
    return m // tm, m % tm


def _for_rows(axis, n_full, tail, body):
    if tail == 0:
        body(slice(None))
        return
    i = pl.program_id(axis)

    @pl.when(i < n_full)
    def _():
        body(slice(None))

    @pl.when(i == n_full)
    def _():
        body(slice(0, tail))


def _proj_kernel(x_ref, w_ref, o_ref, *, n_full, tail):
    def body(rows):
        o_ref[rows, :] = _dot(x_ref[rows, :], _bf(w_ref)).astype(o_ref.dtype)
    _for_rows(1, n_full, tail, body)


def _proj(x, w, layer, col_off, n_cols, out_dtype, tm, tn=COL_TILE):
    m, k = x.shape
    n_full, tail = _row_steps(m, tm)
    off = col_off // tn
    return pl.pallas_call(
        functools.partial(_proj_kernel, n_full=n_full, tail=tail),
        grid=(n_cols // tn, pl.cdiv(m, tm)),
        in_specs=[pl.BlockSpec((tm, k), lambda n, i: (i, 0)),
                  pl.BlockSpec((None, k, tn), lambda n, i: (layer, 0, n + off))],
        out_specs=pl.BlockSpec((tm, tn), lambda n, i: (i, n)),
        out_shape=jax.ShapeDtypeStruct((m, n_cols), out_dtype),
        compiler_params=_params("parallel", "parallel"),
        name="proj",
    )(x, w)


def _kv_slot_kernel(x_ref, w_ref, pbuf_ref, sbuf_ref, op_ref, os_ref, obf_ref, *, n_full, tail):
    del pbuf_ref, sbuf_ref
    n = pl.program_id(1)
    heads_per_tile = obf_ref.shape[1] // SB_HEAD_DIM

    def body(rows, dst, n_rows):
        acc = _dot(x_ref[rows, :], _bf(w_ref))
        obf_ref[rows, :] = acc.astype(BF16)
        for j in range(heads_per_tile):
            dst[0, pl.ds(n * heads_per_tile + j, n_rows, stride=SB_HEADS), :] = (
                acc[:, j * SB_HEAD_DIM:(j + 1) * SB_HEAD_DIM])

    i = pl.program_id(0)

    @pl.when(i < n_full)
    def _():
        body(slice(None), op_ref, x_ref.shape[0])

    @pl.when(i == n_full)
    def _():
        body(slice(0, tail), os_ref, tail)


def _kv_into_slots(pbuf, sbuf, x, w, col_off, layer, tm, tn=COL_TILE):
    m, k = x.shape
    n_full, tail = _row_steps(m, tm)
    off = col_off // tn
    last_full = n_full - 1
    return pl.pallas_call(
        functools.partial(_kv_slot_kernel, n_full=n_full, tail=tail),
        grid=(n_full + 1, SB_WIDTH // tn),
        in_specs=[pl.BlockSpec((tm, k), lambda i, n: (i, 0)),
                  pl.BlockSpec((None, k, tn), lambda i, n: (layer, 0, n + off)),
                  pl.BlockSpec(memory_space=pl.ANY),
                  pl.BlockSpec(memory_space=pl.ANY)],
        out_specs=[pl.BlockSpec((1, tm * SB_HEADS, SB_HEAD_DIM), lambda i, n: (layer, jnp.minimum(i, last_full), 0)),
                   pl.BlockSpec((1, tail * SB_HEADS, SB_HEAD_DIM), lambda i, n: (layer, 0, 0)),
                   pl.BlockSpec((tm, tn), lambda i, n: (i, n))],
        out_shape=[jax.ShapeDtypeStruct(pbuf.shape, F32), jax.ShapeDtypeStruct(sbuf.shape, F32),
                   jax.ShapeDtypeStruct((m, SB_WIDTH), BF16)],
        input_output_aliases={2: 0, 3: 1},
        compiler_params=_params("arbitrary", "arbitrary"),
        name="kv_slot",
    )(x, w, pbuf, sbuf)


def _rot_kernel(x_ref, w_ref, cos_ref, sin_ref, o_ref, *, scale, n_full, tail):
    half = RET_HEAD_DIM // 2

    def body(rows):
        acc = _dot(x_ref[rows, :], _bf(w_ref))
        c = cos_ref[rows, :]
        s = sin_ref[rows, :]
        for j in range(acc.shape[1] // RET_HEAD_DIM):
            lo = j * RET_HEAD_DIM
            x1 = acc[:, lo:lo + half]
            x2 = acc[:, lo + half:lo + RET_HEAD_DIM]
            o_ref[rows, lo:lo + half] = ((x1 * c - x2 * s) * scale).astype(o_ref.dtype)
            o_ref[rows, lo + half:lo + RET_HEAD_DIM] = ((x1 * s + x2 * c) * scale).astype(o_ref.dtype)

    _for_rows(1, n_full, tail, body)


def _proj_rotary(x, w, layer, col_off, cos, sin, scale, out_dtype, tm, tn=COL_TILE):
    m, k = x.shape
    n_full, tail = _row_steps(m, tm)
    off = col_off // tn
    half = RET_HEAD_DIM // 2
    return pl.pallas_call(
        functools.partial(_rot_kernel, scale=scale, n_full=n_full, tail=tail),
        grid=(RET_WIDTH // tn, pl.cdiv(m, tm)),
        in_specs=[pl.BlockSpec((tm, k), lambda n, i: (i, 0)),
                  pl.BlockSpec((None, k, tn), lambda n, i: (layer, 0, n + off)),
                  pl.BlockSpec((tm, half), lambda n, i: (i, 0)),
                  pl.BlockSpec((tm, half), lambda n, i: (i, 0))],
        out_specs=pl.BlockSpec((tm, tn), lambda n, i: (i, n)),
        out_shape=jax.ShapeDtypeStruct((m, RET_WIDTH), out_dtype),
        compiler_params=_params("parallel", "parallel"),
        name="proj_rotary",
    )(x, w, cos, sin)


def _suffix_ones(n):
    row = lax.broadcasted_iota(jnp.int32, (2 * n, n), 0)
    col = lax.broadcasted_iota(jnp.int32, (2 * n, n), 1)
    return jnp.where((row % n) > col, 1.0, 0.0).astype(BF16)


def _sb_tile(q, k, v, bias, later, mask, sub):
    z = _dot_nt(q, k) * (SB_HEAD_DIM ** -0.5) + bias
    t = jnp.log(1.0 + jnp.exp(-jnp.abs(z)))
    softplus = jnp.maximum(z, 0.0) + t
    if mask is not None:
        softplus = jnp.where(mask, softplus, 0.0)
    log_beta = jnp.minimum(z, 0.0) - t
    ones_after = _suffix_ones(sub)
    parts = []
    for j in reversed(range(k.shape[0] // sub)):
        cols = slice(j * sub, (j + 1) * sub)
        s = softplus[:, cols]
        hi = s.astype(BF16)
        lo = (s - hi.astype(F32)).astype(BF16)
        after = _dot(jnp.concatenate([hi, lo], axis=1), ones_after) + later
        parts.append(jnp.exp(log_beta[:, cols] - after))
        later = later + jnp.sum(s, axis=1, keepdims=True)
    a = parts[0] if len(parts) == 1 else jnp.concatenate(parts[::-1], axis=1)
    if mask is not None:
        a = jnp.where(mask, a, 0.0)
    return _dot(a.astype(BF16), v), later


def _sb_prompt_kernel(bias_ref, q_ref, k_ref, v_ref, o_ref, acc_ref):
    qi = pl.program_id(2)
    tile = SB_TILE
    q = q_ref[...]
    bias = bias_ref[0]

    def step(kt, later, mask):
        start = pl.multiple_of(kt * tile, tile)
        return _sb_tile(q, k_ref[pl.ds(start, tile), :], v_ref[pl.ds(start, tile), :], bias, later, mask, SB_SUB)

    row = lax.broadcasted_iota(jnp.int32, (tile, tile), 0)
    col = lax.broadcasted_iota(jnp.int32, (tile, tile), 1)
    o, later = step(qi, jnp.zeros((tile, 1), F32), col < row)
    acc_ref[...] = o

    def body(j, later):
        o, later = step(qi - 1 - j, later, None)
        acc_ref[...] += o
        return later

    lax.fori_loop(0, qi, body, later)
    o_ref[...] = acc_ref[...].astype(o_ref.dtype)


def _sb_prompt(q, k, v, bias, batch, seq):
    tile = SB_TILE
    nq = seq // tile
    bias_b = jnp.broadcast_to(bias.astype(F32)[:, None, None], (SB_HEADS, 1, tile))
    kv_spec = pl.BlockSpec((seq, SB_HEAD_DIM), lambda b, h, i: (b, h))
    return pl.pallas_call(
        _sb_prompt_kernel,
        grid=(batch, SB_HEADS, nq),
        in_specs=[pl.BlockSpec((1, 1, tile), lambda b, h, i: (h, 0, 0)),
                  pl.BlockSpec((tile, SB_HEAD_DIM), lambda b, h, i: (b * nq + i, h)),
                  kv_spec, kv_spec],
        out_specs=pl.BlockSpec((tile, SB_HEAD_DIM), lambda b, h, i: (b * nq + i, h)),
        out_shape=jax.ShapeDtypeStruct((batch * seq, SB_WIDTH), BF16),
        scratch_shapes=[pltpu.VMEM((tile, SB_HEAD_DIM), F32)],
        compiler_params=_params("parallel", "parallel", "arbitrary"),
        name="sb_prompt",
    )(bias_b, q, k, v)


def _heads_to_lanes(page_ref):
    return jnp.concatenate(
        [page_ref[0, 0, pl.ds(h, PAGE_SIZE, stride=SB_HEADS), :].astype(BF16) for h in range(SB_HEADS)], axis=1)


def _sb_decode_kernel(pt_ref, q_ref, bias_ref, kn_ref, vn_ref, *refs):
    del pt_ref
    n = DEC_PAGES_PER_STEP
    k_refs, v_refs = refs[:n], refs[n:2 * n]
    o_ref, acc_ref, later_ref = refs[2 * n:]
    p = pl.program_id(1)
    rows = SB_HEADS * DEC_ROWS_PER_HEAD
    q = q_ref[0]
    bias = bias_ref[...]

    @pl.when(p == 0)
    def _():
        row = lax.broadcasted_iota(jnp.int32, (rows, PAGE_SIZE), 0)
        col = lax.broadcasted_iota(jnp.int32, (rows, PAGE_SIZE), 1)
        o, later = _sb_tile(q, kn_ref[0], vn_ref[0], bias,
                            jnp.zeros((rows, 1), F32), col < (row % DEC_ROWS_PER_HEAD), PAGE_SIZE)
        acc_ref[...] = o
        later_ref[...] = later

    k = jnp.concatenate([_heads_to_lanes(r) for r in k_refs[::-1]], axis=0)
    v = jnp.concatenate([_heads_to_lanes(r) for r in v_refs[::-1]], axis=0)
    o, later = _sb_tile(q, k, v, bias, later_ref[...], None, SB_SUB)
    acc_ref[...] += o
    later_ref[...] = later

    @pl.when(p == pl.num_programs(1) - 1)
    def _():
        r = DEC_ROWS_PER_HEAD
        for h in range(SB_HEADS):
            cols = slice(h * SB_HEAD_DIM, (h + 1) * SB_HEAD_DIM)
            o_ref[0, :, cols] = acc_ref[h * r:(h + 1) * r, cols]


def _sb_decode(q, k_new, v_new, cache_k, cache_v, page_table, bias, layer):
    bs, n_pages = page_table.shape
    ts = q.shape[0] // bs
    r = DEC_ROWS_PER_HEAD
    rows = SB_HEADS * r
    n = DEC_PAGES_PER_STEP
    q4 = jnp.pad(q.reshape(bs, ts, SB_HEADS, SB_HEAD_DIM), ((0, 0), (0, r - ts), (0, 0), (0, 0)))
    q4 = q4.transpose(0, 2, 1, 3)
    eye = jnp.eye(SB_HEADS, dtype=jnp.bool_)[None, :, None, :, None]
    qbd = jnp.where(eye, q4[:, :, :, None, :], jnp.zeros((), q.dtype)).reshape(bs, rows, SB_WIDTH)
    pad_new = lambda a: jnp.pad(a.reshape(bs, ts, SB_WIDTH), ((0, 0), (0, PAGE_SIZE - ts), (0, 0)))
    bias_rows = jnp.repeat(bias.astype(F32), r)[:, None]
    pt_flat = page_table.reshape(-1)

    def page_spec(slot):
        def page_map(b, p, pt):
            return (layer, pt[b * n_pages + (n_pages - 1 - (p * n + slot))], 0, 0)
        return pl.BlockSpec((1, 1, PAGE_SIZE * SB_HEADS, SB_HEAD_DIM), page_map)

    new_spec = pl.BlockSpec((1, PAGE_SIZE, SB_WIDTH), lambda b, p, pt: (b, 0, 0))
    grid_spec = pltpu.PrefetchScalarGridSpec(
        num_scalar_prefetch=1,
        grid=(bs, n_pages // n),
        in_specs=[pl.BlockSpec((1, rows, SB_WIDTH), lambda b, p, pt: (b, 0, 0)),
                  pl.BlockSpec((rows, 1), lambda b, p, pt: (0, 0)),
                  new_spec, new_spec] + [page_spec(s) for s in range(n)] * 2,
        out_specs=pl.BlockSpec((1, r, SB_WIDTH), lambda b, p, pt: (b, 0, 0)),
        scratch_shapes=[pltpu.VMEM((rows, SB_WIDTH), F32), pltpu.VMEM((rows, 1), F32)],
    )
    o = pl.pallas_call(
        _sb_decode_kernel,
        grid_spec=grid_spec,
        out_shape=jax.ShapeDtypeStruct((bs, r, SB_WIDTH), F32),
        compiler_params=_params("parallel", "arbitrary"),
        name="sb_decode",
    )(pt_flat, qbd, bias_rows, pad_new(k_new), pad_new(v_new), *([cache_k] * n), *([cache_v] * n))
    return o[:, :ts].reshape(bs * ts, SB_WIDTH).astype(BF16)


def _ret_kernel(*refs, has_s0):
    if has_s0:
        q_ref, k_ref, v_ref, g_ref, gng_ref, gnb_ref, dm_ref, qd_ref, kd_ref, cd_ref, s0_ref, y_ref, so_ref, s_ref = refs
    else:
        q_ref, k_ref, v_ref, g_ref, gng_ref, gnb_ref, dm_ref, qd_ref, kd_ref, cd_ref, y_ref, so_ref, s_ref = refs
    c = pl.program_id(2)

    @pl.when(c == 0)
    def _():
        if has_s0:
            s_ref[...] = s0_ref[0, 0]
        else:
            s_ref[...] = jnp.zeros_like(s_ref)

    q = q_ref[...]
    k = k_ref[...]
    v = v_ref[...]
    s = s_ref[...]
    inner = _dot_nt(q, k.astype(BF16)) * dm_ref[0]
    o = _dot(inner.astype(BF16), v) + _dot(q, s.astype(BF16)) * qd_ref[0]
    s_new = cd_ref[0] * s + _dot_tn((k * kd_ref[0]).astype(BF16), v)
    s_ref[...] = s_new

    mu = jnp.mean(o, axis=-1, keepdims=True)
    d = o - mu
    var = jnp.mean(d * d, axis=-1, keepdims=True)
    y = d * lax.rsqrt(var + GN_EPS) * gng_ref[...] + gnb_ref[...]
    g = g_ref[...]
    y_ref[...] = (g * jax.nn.sigmoid(g) * y).astype(y_ref.dtype)

    @pl.when(c == pl.num_programs(2) - 1)
    def _():
        so_ref[0, 0] = s_new


def _retention_tables(c_real, c_pad):
    lg = jnp.log1p(-jnp.exp2(-5.0 - jnp.arange(RET_HEADS, dtype=F32)))
    idx = jnp.arange(c_pad, dtype=F32)
    real = idx < c_real
    diff = idx[:, None] - idx[None, :]
    dmask = jnp.where(diff >= 0, jnp.exp(lg[:, None, None] * jnp.maximum(diff, 0.0)), 0.0)
    dmask = jnp.where(real[None, :, None] & real[None, None, :], dmask, 0.0)
    qdec = jnp.exp(lg[:, None] * (idx[None, :] + 1.0))
    kdec = jnp.where(real[None, :], jnp.exp(lg[:, None] * (c_real - 1.0 - idx[None, :])), 0.0)
    cdec = jnp.exp(lg * c_real)
    wide = lambda a: jnp.broadcast_to(a[:, :, None], (RET_HEADS, c_pad, RET_HEAD_DIM))
    return dmask, wide(qdec), wide(kdec), jnp.broadcast_to(cdec[:, None, None], (RET_HEADS, 1, RET_HEAD_DIM))


def _retention(q, k, v, gsw, gn_g, gn_b, s0, batch, n_chunks, c_real, c_pad):
    dmask, qdec, kdec, cdec = _retention_tables(c_real, c_pad)
    d = RET_HEAD_DIM
    row_spec = pl.BlockSpec((c_pad, d), lambda b, h, c: (b * n_chunks + c, h))
    head_vec = pl.BlockSpec((1, d), lambda b, h, c: (0, h))
    table = lambda r: pl.BlockSpec((1, r, d), lambda b, h, c: (h, 0, 0))
    state_spec = pl.BlockSpec((1, 1, d, d), lambda b, h, c: (b, h, 0, 0))
    in_specs = [row_spec, row_spec, row_spec, row_spec, head_vec, head_vec,
                pl.BlockSpec((1, c_pad, c_pad), lambda b, h, c: (h, 0, 0)), table(c_pad), table(c_pad), table(1)]
    args = [q, k, v, gsw, gn_g.reshape(1, -1).astype(F32), gn_b.reshape(1, -1).astype(F32), dmask, qdec, kdec, cdec]
    if s0 is not None:
        in_specs.append(state_spec)
        args.append(s0)
    return pl.pallas_call(
        functools.partial(_ret_kernel, has_s0=s0 is not None),
        grid=(batch, RET_HEADS, n_chunks),
        in_specs=in_specs,
        out_specs=[row_spec, state_spec],
        out_shape=[jax.ShapeDtypeStruct((batch * n_chunks * c_pad, RET_WIDTH), BF16),
                   jax.ShapeDtypeStruct((batch, RET_HEADS, d, d), F32)],
        scratch_shapes=[pltpu.VMEM((d, d), F32)],
        compiler_params=_params("parallel", "parallel", "arbitrary"),
        name="retention",
    )(*args)


def _merge_kernel(oa_ref, yb_ref, oas_ref, ybs_ref, wa_ref, wb_ref, ga_ref, gb_ref, o_ref, *, n_full, tail):
    def body(rows):
        oa, yb = (oa_ref, yb_ref) if rows == slice(None) else (oas_ref, ybs_ref)
        a = _dot(oa[...], _bf(wa_ref))
        b = _dot(yb[...], _bf(wb_ref))
        o_ref[rows, :] = (jax.nn.sigmoid(ga_ref[rows, :]) * a
                          + jax.nn.sigmoid(gb_ref[rows, :]) * b).astype(o_ref.dtype)
    _for_rows(1, n_full, tail, body)


def _merge(oa, yb, oa_s, yb_s, wa, wb, layer, gates, tm, tn=COL_TILE):
    k = oa.shape[1]
    m = gates.shape[0]
    n_full, tail = _row_steps(m, tm)
    assert oa.shape[0] == n_full * tm and oa_s.shape[0] == tail
    gb_off = D_MODEL // tn
    last_full = n_full - 1
    prompt_rows = pl.BlockSpec((tm, k), lambda n, i: (jnp.minimum(i, last_full), 0))
    sample_rows = pl.BlockSpec((tail, k), lambda n, i: (0, 0))
    return pl.pallas_call(
        functools.partial(_merge_kernel, n_full=n_full, tail=tail),
        grid=(D_MODEL // tn, pl.cdiv(m, tm)),
        in_specs=[prompt_rows, prompt_rows, sample_rows, sample_rows,
                  pl.BlockSpec((None, k, tn), lambda n, i: (layer, 0, n)),
                  pl.BlockSpec((None, k, tn), lambda n, i: (layer, 0, n)),
                  pl.BlockSpec((tm, tn), lambda n, i: (i, n)),
                  pl.BlockSpec((tm, tn), lambda n, i: (i, n + gb_off))],
        out_specs=pl.BlockSpec((tm, tn), lambda n, i: (i, n)),
        out_shape=jax.ShapeDtypeStruct((m, D_MODEL), BF16),
        compiler_params=_params("parallel", "parallel"),
        name="merge",
    )(oa, yb, oa_s, yb_s, wa, wb, gates, gates)


def _resid_kernel(a_ref, w_ref, x_ref, o_ref, *, n_full, tail):
    def body(rows):
        o_ref[rows, :] = ALPHA * x_ref[rows, :] + _dot(a_ref[rows, :], _bf(w_ref))
    _for_rows(1, n_full, tail, body)


def _resid_proj(a, w, layer, x, tm, tn=COL_TILE):
    m, k = a.shape
    n_full, tail = _row_steps(m, tm)
    return pl.pallas_call(
        functools.partial(_resid_kernel, n_full=n_full, tail=tail),
        grid=(D_MODEL // tn, pl.cdiv(m, tm)),
        in_specs=[pl.BlockSpec((tm, k), lambda n, i: (i, 0)),
                  pl.BlockSpec((None, k, tn), lambda n, i: (layer, 0, n)),
                  pl.BlockSpec((tm, tn), lambda n, i: (i, n))],
        out_specs=pl.BlockSpec((tm, tn), lambda n, i: (i, n)),
        out_shape=jax.ShapeDtypeStruct((m, D_MODEL), F32),
        compiler_params=_params("parallel", "parallel"),
        name="resid_proj",
    )(a, w, x)


def _ln_kernel(x_ref, g_ref, b_ref, o_ref, obf_ref, *, n_full, tail):
    def body(rows):
        x = x_ref[rows, :]
        mu = jnp.mean(x, axis=-1, keepdims=True)
        d = x - mu
        var = jnp.mean(d * d, axis=-1, keepdims=True)
        y = d * lax.rsqrt(var + LN_EPS) * g_ref[...] + b_ref[...]
        o_ref[rows, :] = y
        obf_ref[rows, :] = y.astype(BF16)
    _for_rows(0, n_full, tail, body)


def _layer_norm(x, g, b, tm):
    m, n = x.shape
    n_full, tail = _row_steps(m, tm)
    row = pl.BlockSpec((tm, n), lambda i: (i, 0))
    vec = pl.BlockSpec((1, n), lambda i: (0, 0))
    return pl.pallas_call(
        functools.partial(_ln_kernel, n_full=n_full, tail=tail),
        grid=(pl.cdiv(m, tm),),
        in_specs=[row, vec, vec],
        out_specs=[row, row],
        out_shape=[jax.ShapeDtypeStruct((m, n), F32), jax.ShapeDtypeStruct((m, n), BF16)],
        compiler_params=_params("parallel"),
        name="layer_norm",
    )(x, g.reshape(1, n).astype(F32), b.reshape(1, n).astype(F32))


def _swiglu_kernel(x_ref, wg_ref, wu_ref, o_ref, *, n_full, tail):
    def body(rows):
        x = x_ref[rows, :]
        g = _dot(x, _bf(wg_ref))
        u = _dot(x, _bf(wu_ref))
        o_ref[rows, :] = (g * jax.nn.sigmoid(g) * u).astype(o_ref.dtype)
    _for_rows(1, n_full, tail, body)


def _swiglu(x, w, layer, tm, tn=MXU_WIDTH):
    m, k = x.shape
    n_full, tail = _row_steps(m, tm)
    up_off = D_FF // tn
    return pl.pallas_call(
        functools.partial(_swiglu_kernel, n_full=n_full, tail=tail),
        grid=(D_FF // tn, pl.cdiv(m, tm)),
        in_specs=[pl.BlockSpec((tm, k), lambda n, i: (i, 0)),
                  pl.BlockSpec((None, k, tn), lambda n, i: (layer, 0, n)),
                  pl.BlockSpec((None, k, tn), lambda n, i: (layer, 0, n + up_off))],
        out_specs=pl.BlockSpec((tm, tn), lambda n, i: (i, n)),
        out_shape=jax.ShapeDtypeStruct((m, D_FF), BF16),
        compiler_params=_params("parallel", "parallel"),
        name="swiglu",
    )(x, w, w)


def _ple_kernel(xbf_ref, wg_ref, p_ref, wp_ref, x_ref, o_ref, *, n_full, tail):
    def body(rows):
        gate = jax.nn.sigmoid(_dot(xbf_ref[rows, :], _bf(wg_ref)))
        o_ref[rows, :] = ALPHA * x_ref[rows, :] + gate * _dot(p_ref[rows, :], _bf(wp_ref))
    _for_rows(1, n_full, tail, body)


def _ple(xbf, wg, p, wp, layer, x, tm, tn=COL_TILE):
    m, k = xbf.shape
    n_full, tail = _row_steps(m, tm)
    return pl.pallas_call(
        functools.partial(_ple_kernel, n_full=n_full, tail=tail),
        grid=(D_MODEL // tn, pl.cdiv(m, tm)),
        in_specs=[pl.BlockSpec((tm, k), lambda n, i: (i, 0)),
                  pl.BlockSpec((None, k, tn), lambda n, i: (layer, 0, n)),
                  pl.BlockSpec((None, tm, PLE_DIM), lambda n, i: (layer, i, 0)),
                  pl.BlockSpec((None, PLE_DIM, tn), lambda n, i: (layer, 0, n)),
                  pl.BlockSpec((tm, tn), lambda n, i: (i, n))],
        out_specs=pl.BlockSpec((tm, tn), lambda n, i: (i, n)),
        out_shape=jax.ShapeDtypeStruct((m, D_MODEL), F32),
        compiler_params=_params("parallel", "parallel"),
        name="ple",
    )(xbf, wg, p, wp, x)


def _ffn_out_kernel(a_ref, w_ref, r_ref, o_ref, *, n_full, tail):
    k = pl.program_id(2)

    def body(rows):
        d = _dot(a_ref[rows, :], w_ref[...])

        @pl.when(k == 0)
        def _():
            o_ref[rows, :] = r_ref[rows, :] + d

        @pl.when(k > 0)
        def _():
            o_ref[rows, :] += d

    _for_rows(1, n_full, tail, body)


def _ffn_out(a, w, layer, r, tm, tn=COL_TILE, nk=2):
    m, k = a.shape
    n_full, tail = _row_steps(m, tm)
    tk = k // nk
    return pl.pallas_call(
        functools.partial(_ffn_out_kernel, n_full=n_full, tail=tail),
        grid=(D_MODEL // tn, pl.cdiv(m, tm), nk),
        in_specs=[pl.BlockSpec((tm, tk), lambda n, i, j: (i, j)),
                  pl.BlockSpec((None, tk, tn), lambda n, i, j: (layer, j, n)),
                  pl.BlockSpec((tm, tn), lambda n, i, j: (i, n))],
        out_specs=pl.BlockSpec((tm, tn), lambda n, i, j: (i, n)),
        out_shape=jax.ShapeDtypeStruct((m, D_MODEL), F32),
        compiler_params=_params("parallel", "parallel", "arbitrary"),
        name="ffn_out",
    )(a, w, r)


def _rotary_tables(pos):
    theta = ROPE_BASE ** (-jnp.arange(0, RET_HEAD_DIM, 2, dtype=F32) / RET_HEAD_DIM)
    ang = pos.astype(F32)[:, None] * theta[None, :]
    return jnp.cos(ang), jnp.sin(ang)


def _pad_rows(a, batch, t, t_pad):
    return jnp.pad(a.reshape(batch, t, -1), ((0, 0), (0, t_pad - t), (0, 0))).reshape(batch * t_pad, -1)


def kernel(x_prompt, x_sample, cache_sb_k, cache_sb_v, state_ret, page_table, p_prompt, p_sample,
           w_in, b_sb, gn_g, gn_b, w_proj_a, w_proj_b, w_out, ln1_g, ln1_b,
           w_ffn_in, w_ffn_out, w_ple_gate, w_ple_proj, ln2_g, ln2_b):
    bp, tp, _ = x_prompt.shape
    bs, ts, _ = x_sample.shape
    mp, ms = bp * tp, bs * ts
    n_pool = cache_sb_k.shape[1]
    past = page_table.shape[1] * PAGE_SIZE
    cache_k = cache_sb_k.reshape(DEPTH, n_pool, PAGE_SIZE * SB_HEADS, SB_HEAD_DIM)
    cache_v = cache_sb_v.reshape(DEPTH, n_pool, PAGE_SIZE * SB_HEADS, SB_HEAD_DIM)

    pos = jnp.concatenate([jnp.tile(jnp.arange(tp), bp), jnp.tile(past + jnp.arange(ts), bs)])
    cos, sin = _rotary_tables(pos)
    ts_pad = 16
    pad = lambda a: _pad_rows(a[mp:], bs, ts, ts_pad)

    x = jnp.concatenate([x_prompt.reshape(mp, D_MODEL), x_sample.reshape(ms, D_MODEL)], axis=0)
    xbf = x.astype(BF16)
    p_all = jnp.concatenate([p_prompt.reshape(DEPTH, mp, PLE_DIM), p_sample.reshape(DEPTH, ms, PLE_DIM)],
                            axis=1).astype(BF16)
    w_ffn_out_bf = w_ffn_out.astype(BF16)
    kp = jnp.zeros((DEPTH, mp * SB_HEADS, SB_HEAD_DIM), F32)
    vp = jnp.zeros((DEPTH, mp * SB_HEADS, SB_HEAD_DIM), F32)
    ks = jnp.zeros((DEPTH, ms * SB_HEADS, SB_HEAD_DIM), F32)
    vs = jnp.zeros((DEPTH, ms * SB_HEADS, SB_HEAD_DIM), F32)
    sp_l, ss_l = [], []
    tm = min(ROW_TILE, mp)
    tm_wide = min(2 * ROW_TILE, mp)
    tm_norm = min(256, mp)
    for i in range(DEPTH):
        kp, ks, k_a = _kv_into_slots(kp, ks, xbf, w_in, OFF_KA, i, tm)
        vp, vs, v_a = _kv_into_slots(vp, vs, xbf, w_in, OFF_VA, i, tm)
        q_a = _proj(xbf, w_in, i, OFF_QA, SB_WIDTH, BF16, tm)
        q_b = _proj_rotary(xbf, w_in, i, OFF_QB, cos, sin, 1.0, BF16, tm)
        k_b = _proj_rotary(xbf, w_in, i, OFF_KB, cos, sin, RET_HEAD_DIM ** -0.5, F32, tm)
        v_b = _proj(xbf, w_in, i, OFF_VB, RET_WIDTH, BF16, tm)
        gsw = _proj(xbf, w_in, i, OFF_GSW, RET_WIDTH, F32, tm)
        gates = _proj(xbf, w_in, i, OFF_G, 2 * D_MODEL, F32, tm)

        oa_p = _sb_prompt(q_a, k_a, v_a, b_sb[i], bp, tp)
        yb_p, s_p = _retention(q_b, k_b, v_b, gsw, gn_g[i], gn_b[i], None, bp, tp // RET_BLOCK, RET_BLOCK, RET_BLOCK)
        oa_s = _sb_decode(q_a[mp:], k_a[mp:], v_a[mp:], cache_k, cache_v, page_table, b_sb[i], i)
        yb_s, s_s = _retention(pad(q_b), pad(k_b), pad(v_b), pad(gsw), gn_g[i], gn_b[i], state_ret[i],
                               bs, 1, ts, ts_pad)
        yb_s = yb_s.reshape(bs, ts_pad, RET_WIDTH)[:, :ts].reshape(ms, RET_WIDTH)
        sp_l.append(s_p)
        ss_l.append(s_s)

        merged = _merge(oa_p, yb_p, oa_s, yb_s, w_proj_a, w_proj_b, i, gates, tm)
        x1, x1bf = _layer_norm(_resid_proj(merged, w_out, i, x, tm), ln1_g[i], ln1_b[i], tm_norm)
        act = _swiglu(x1bf, w_ffn_in, i, tm_wide)
        side = _ple(x1bf, w_ple_gate, p_all, w_ple_proj, i, x1, tm)
        x, xbf = _layer_norm(_ffn_out(act, w_ffn_out_bf, i, side, tm), ln2_g[i], ln2_b[i], tm_norm)

    heads = lambda a, b, t: a.reshape(DEPTH, b, t, SB_HEADS, SB_HEAD_DIM)
    return (x[:mp].reshape(bp, tp, D_MODEL), x[mp:].reshape(bs, ts, D_MODEL),
            heads(kp, bp, tp), heads(vp, bp, tp), heads(ks, bs, ts), heads(vs, bs, ts),
            jnp.stack(sp_l), jnp.stack(ss_l))
```

```python
import functools

import jax
import jax.numpy as jnp
from jax import lax
from jax.experimental import pallas as pl
from jax.experimental.pallas import tpu as pltpu

F32 = jnp.float32
BF16 = jnp.bfloat16

D_MODEL = 4096
DEPTH = 4
PAGE_SIZE = 128
SB_HEADS = 16
SB_HEAD_DIM = 128
SB_WIDTH = SB_HEADS * SB_HEAD_DIM
RET_HEADS = 8
RET_HEAD_DIM = 256
RET_WIDTH = RET_HEADS * RET_HEAD_DIM
D_FF = 11008
PLE_DIM = 256
ALPHA = (2 * DEPTH) ** 0.25
LN_EPS = 1e-5
GN_EPS = 1e-6
ROPE_BASE = 10000.0

OFF_QA, OFF_KA, OFF_VA, OFF_QB, OFF_KB, OFF_VB, OFF_GSW, OFF_G = 0, 2048, 4096, 6144, 8192, 10240, 12288, 14336

V7X_VMEM_LIMIT_BYTES = 56 * 1024 * 1024
MXU_WIDTH = 256
ROW_TILE = 1024
COL_TILE = 1024
COL_TILE_SMALL = 512
SB_TILE = 512
SB_SUB = MXU_WIDTH
RET_BLOCK = MXU_WIDTH
DEC_ROWS_PER_HEAD = 8
DEC_PAGES_PER_STEP = SB_TILE // PAGE_SIZE


def _params(*sem):
    return pltpu.CompilerParams(dimension_semantics=sem, vmem_limit_bytes=V7X_VMEM_LIMIT_BYTES)


def _dot(a, b):
    return jnp.dot(a, b, preferred_element_type=F32)


def _dot_nt(a, b):
    return lax.dot_general(a, b, (((1,), (1,)), ((), ())), preferred_element_type=F32)


def _dot_tn(a, b):
    return lax.dot_general(a, b, (((0,), (0,)), ((), ())), preferred_element_type=F32)


def _row_steps(m, tm):
    return m // tm, m % tm


def _for_rows(axis, n_full, tail, body):
    if tail == 0:
        body(slice(None))
        return
    i = pl.program_id(axis)

    @pl.when(i < n_full)
    def _():
        body(slice(None))

    @pl.when(i == n_full)
    def _():
        body(slice(0, tail))


def _proj_kernel(x_ref, w_ref, o_ref, *, n_full, tail):
    def body(rows):
        o_ref[rows, :] = _dot(x_ref[rows, :], w_ref[...]).astype(o_ref.dtype)
    _for_rows(1, n_full, tail, body)


def _proj(x, w, layer, col_off, n_cols, out_dtype, tm, tn=COL_TILE):
    m, k = x.shape
    n_full, tail = _row_steps(m, tm)
    off = col_off // tn
    return pl.pallas_call(
        functools.partial(_proj_kernel, n_full=n_full, tail=tail),
        grid=(n_cols // tn, pl.cdiv(m, tm)),
        in_specs=[pl.BlockSpec((tm, k), lambda n, i: (i, 0)),
                  pl.BlockSpec((None, k, tn), lambda n, i: (layer, 0, n + off))],
        out_specs=pl.BlockSpec((tm, tn), lambda n, i: (i, n)),
        out_shape=jax.ShapeDtypeStruct((m, n_cols), out_dtype),
        compiler_params=_params("parallel", "parallel"),
        name="proj",
    )(x, w)


def _kv_slot_kernel(x_ref, w_ref, pbuf_ref, sbuf_ref, op_ref, os_ref, obf_ref, *, n_full, tail):
    del pbuf_ref, sbuf_ref
    n = pl.program_id(1)
    heads_per_tile = obf_ref.shape[1] // SB_HEAD_DIM

    def body(rows, dst, n_rows):
        acc = _dot(x_ref[rows, :], w_ref[...])
        obf_ref[rows, :] = acc.astype(BF16)
        for j in range(heads_per_tile):
            dst[0, pl.ds(n * heads_per_tile + j, n_rows, stride=SB_HEADS), :] = (
                acc[:, j * SB_HEAD_DIM:(j + 1) * SB_HEAD_DIM])

    i = pl.program_id(0)

    @pl.when(i < n_full)
    def _():
        body(slice(None), op_ref, x_ref.shape[0])

    @pl.when(i == n_full)
    def _():
        body(slice(0, tail), os_ref, tail)


def _kv_into_slots(pbuf, sbuf, x, w, col_off, layer, tm, tn=COL_TILE_SMALL):
    m, k = x.shape
    n_full, tail = _row_steps(m, tm)
    off = col_off // tn
    last_full = n_full - 1
    return pl.pallas_call(
        functools.partial(_kv_slot_kernel, n_full=n_full, tail=tail),
        grid=(n_full + 1, SB_WIDTH // tn),
        in_specs=[pl.BlockSpec((tm, k), lambda i, n: (i, 0)),
                  pl.BlockSpec((None, k, tn), lambda i, n: (layer, 0, n + off)),
                  pl.BlockSpec(memory_space=pl.ANY),
                  pl.BlockSpec(memory_space=pl.ANY)],
        out_specs=[pl.BlockSpec((1, tm * SB_HEADS, SB_HEAD_DIM), lambda i, n: (layer, jnp.minimum(i, last_full), 0)),
                   pl.BlockSpec((1, tail * SB_HEADS, SB_HEAD_DIM), lambda i, n: (layer, 0, 0)),
                   pl.BlockSpec((tm, tn), lambda i, n: (i, n))],
        out_shape=[jax.ShapeDtypeStruct(pbuf.shape, F32), jax.ShapeDtypeStruct(sbuf.shape, F32),
                   jax.ShapeDtypeStruct((m, SB_WIDTH), BF16)],
        input_output_aliases={2: 0, 3: 1},
        compiler_params=_params("arbitrary", "arbitrary"),
        name="kv_slot",
    )(x, w, pbuf, sbuf)


def _rot_kernel(x_ref, w_ref, cos_ref, sin_ref, o_ref, *, scale, n_full, tail):
    half = RET_HEAD_DIM // 2

    def body(rows):
        acc = _dot(x_ref[rows, :], w_ref[...])
        c = cos_ref[rows, :]
        s = sin_ref[rows, :]
        for j in range(acc.shape[1] // RET_HEAD_DIM):
            lo = j * RET_HEAD_DIM
            x1 = acc[:, lo:lo + half]
            x2 = acc[:, lo + half:lo + RET_HEAD_DIM]
            o_ref[rows, lo:lo + half] = ((x1 * c - x2 * s) * scale).astype(o_ref.dtype)
            o_ref[rows, lo + half:lo + RET_HEAD_DIM] = ((x1 * s + x2 * c) * scale).astype(o_ref.dtype)

    _for_rows(1, n_full, tail, body)


def _proj_rotary(x, w, layer, col_off, cos, sin, scale, out_dtype, tm, tn=COL_TILE):
    m, k = x.shape
    n_full, tail = _row_steps(m, tm)
    off = col_off // tn
    half = RET_HEAD_DIM // 2
    return pl.pallas_call(
        functools.partial(_rot_kernel, scale=scale, n_full=n_full, tail=tail),
        grid=(RET_WIDTH // tn, pl.cdiv(m, tm)),
        in_specs=[pl.BlockSpec((tm, k), lambda n, i: (i, 0)),
                  pl.BlockSpec((None, k, tn), lambda n, i: (layer, 0, n + off)),
                  pl.BlockSpec((tm, half), lambda n, i: (i, 0)),
                  pl.BlockSpec((tm, half), lambda n, i: (i, 0))],
        out_specs=pl.BlockSpec((tm, tn), lambda n, i: (i, n)),
        out_shape=jax.ShapeDtypeStruct((m, RET_WIDTH), out_dtype),
        compiler_params=_params("parallel", "parallel"),
        name="proj_rotary",
    )(x, w, cos, sin)


def _suffix_ones(n):
    row = lax.broadcasted_iota(jnp.int32, (2 * n, n), 0)
    col = lax.broadcasted_iota(jnp.int32, (2 * n, n), 1)
    return jnp.where((row % n) > col, 1.0, 0.0).astype(BF16)


def _sb_tile(q, k, v, bias, later, mask, sub):
    z = _dot_nt(q, k) * (SB_HEAD_DIM ** -0.5) + bias
    t = jnp.log(1.0 + jnp.exp(-jnp.abs(z)))
    softplus = jnp.maximum(z, 0.0) + t
    if mask is not None:
        softplus = jnp.where(mask, softplus, 0.0)
    log_beta = jnp.minimum(z, 0.0) - t
    ones_after = _suffix_ones(sub)
    parts = []
    for j in reversed(range(k.shape[0] // sub)):
        cols = slice(j * sub, (j + 1) * sub)
        s = softplus[:, cols]
        hi = s.astype(BF16)
        lo = (s - hi.astype(F32)).astype(BF16)
        after = _dot(jnp.concatenate([hi, lo], axis=1), ones_after) + later
        parts.append(jnp.exp(log_beta[:, cols] - after))
        later = later + jnp.sum(s, axis=1, keepdims=True)
    a = parts[0] if len(parts) == 1 else jnp.concatenate(parts[::-1], axis=1)
    if mask is not None:
        a = jnp.where(mask, a, 0.0)
    return _dot(a.astype(BF16), v), later


def _sb_prompt_kernel(bias_ref, q_ref, k_ref, v_ref, o_ref, acc_ref):
    qi = pl.program_id(2)
    tile = SB_TILE
    q = q_ref[...]
    bias = bias_ref[0]

    def step(kt, later, mask):
        start = pl.multiple_of(kt * tile, tile)
        return _sb_tile(q, k_ref[pl.ds(start, tile), :], v_ref[pl.ds(start, tile), :], bias, later, mask, SB_SUB)

    row = lax.broadcasted_iota(jnp.int32, (tile, tile), 0)
    col = lax.broadcasted_iota(jnp.int32, (tile, tile), 1)
    o, later = step(qi, jnp.zeros((tile, 1), F32), col < row)
    acc_ref[...] = o

    def body(j, later):
        o, later = step(qi - 1 - j, later, None)
        acc_ref[...] += o
        return later

    lax.fori_loop(0, qi, body, later)
    o_ref[...] = acc_ref[...].astype(o_ref.dtype)


def _sb_prompt(q, k, v, bias, batch, seq):
    tile = SB_TILE
    nq = seq // tile
    bias_b = jnp.broadcast_to(bias.astype(F32)[:, None, None], (SB_HEADS, 1, tile))
    kv_spec = pl.BlockSpec((seq, SB_HEAD_DIM), lambda b, h, i: (b, h))
    return pl.pallas_call(
        _sb_prompt_kernel,
        grid=(batch, SB_HEADS, nq),
        in_specs=[pl.BlockSpec((1, 1, tile), lambda b, h, i: (h, 0, 0)),
                  pl.BlockSpec((tile, SB_HEAD_DIM), lambda b, h, i: (b * nq + i, h)),
                  kv_spec, kv_spec],
        out_specs=pl.BlockSpec((tile, SB_HEAD_DIM), lambda b, h, i: (b * nq + i, h)),
        out_shape=jax.ShapeDtypeStruct((batch * seq, SB_WIDTH), BF16),
        scratch_shapes=[pltpu.VMEM((tile, SB_HEAD_DIM), F32)],
        compiler_params=_params("parallel", "parallel", "arbitrary"),
        name="sb_prompt",
    )(bias_b, q, k, v)


def _heads_to_lanes(page_ref):
    return jnp.concatenate(
        [page_ref[0, 0, pl.ds(h, PAGE_SIZE, stride=SB_HEADS), :].astype(BF16) for h in range(SB_HEADS)], axis=1)


def _sb_decode_kernel(pt_ref, q_ref, bias_ref, kn_ref, vn_ref, *refs):
    del pt_ref
    n = DEC_PAGES_PER_STEP
    k_refs, v_refs = refs[:n], refs[n:2 * n]
    o_ref, acc_ref, later_ref = refs[2 * n:]
    p = pl.program_id(1)
    rows = SB_HEADS * DEC_ROWS_PER_HEAD
    q = q_ref[0]
    bias = bias_ref[...]

    @pl.when(p == 0)
    def _():
        row = lax.broadcasted_iota(jnp.int32, (rows, PAGE_SIZE), 0)
        col = lax.broadcasted_iota(jnp.int32, (rows, PAGE_SIZE), 1)
        o, later = _sb_tile(q, kn_ref[0], vn_ref[0], bias,
                            jnp.zeros((rows, 1), F32), col < (row % DEC_ROWS_PER_HEAD), PAGE_SIZE)
        acc_ref[...] = o
        later_ref[...] = later

    k = jnp.concatenate([_heads_to_lanes(r) for r in k_refs[::-1]], axis=0)
    v = jnp.concatenate([_heads_to_lanes(r) for r in v_refs[::-1]], axis=0)
    o, later = _sb_tile(q, k, v, bias, later_ref[...], None, SB_SUB)
    acc_ref[...] += o
    later_ref[...] = later

    @pl.when(p == pl.num_programs(1) - 1)
    def _():
        r = DEC_ROWS_PER_HEAD
        for h in range(SB_HEADS):
            cols = slice(h * SB_HEAD_DIM, (h + 1) * SB_HEAD_DIM)
            o_ref[0, :, cols] = acc_ref[h * r:(h + 1) * r, cols]


def _sb_decode(q, k_new, v_new, cache_k, cache_v, page_table, bias, layer):
    bs, n_pages = page_table.shape
    ts = q.shape[0] // bs
    r = DEC_ROWS_PER_HEAD
    rows = SB_HEADS * r
    n = DEC_PAGES_PER_STEP
    q4 = jnp.pad(q.reshape(bs, ts, SB_HEADS, SB_HEAD_DIM), ((0, 0), (0, r - ts), (0, 0), (0, 0)))
    q4 = q4.transpose(0, 2, 1, 3)
    eye = jnp.eye(SB_HEADS, dtype=jnp.bool_)[None, :, None, :, None]
    qbd = jnp.where(eye, q4[:, :, :, None, :], jnp.zeros((), q.dtype)).reshape(bs, rows, SB_WIDTH)
    pad_new = lambda a: jnp.pad(a.reshape(bs, ts, SB_WIDTH), ((0, 0), (0, PAGE_SIZE - ts), (0, 0)))
    bias_rows = jnp.repeat(bias.astype(F32), r)[:, None]
    pt_flat = page_table.reshape(-1)

    def page_spec(slot):
        def page_map(b, p, pt):
            return (layer, pt[b * n_pages + (n_pages - 1 - (p * n + slot))], 0, 0)
        return pl.BlockSpec((1, 1, PAGE_SIZE * SB_HEADS, SB_HEAD_DIM), page_map)

    new_spec = pl.BlockSpec((1, PAGE_SIZE, SB_WIDTH), lambda b, p, pt: (b, 0, 0))
    grid_spec = pltpu.PrefetchScalarGridSpec(
        num_scalar_prefetch=1,
        grid=(bs, n_pages // n),
        in_specs=[pl.BlockSpec((1, rows, SB_WIDTH), lambda b, p, pt: (b, 0, 0)),
                  pl.BlockSpec((rows, 1), lambda b, p, pt: (0, 0)),
                  new_spec, new_spec] + [page_spec(s) for s in range(n)] * 2,
        out_specs=pl.BlockSpec((1, r, SB_WIDTH), lambda b, p, pt: (b, 0, 0)),
        scratch_shapes=[pltpu.VMEM((rows, SB_WIDTH), F32), pltpu.VMEM((rows, 1), F32)],
    )
    o = pl.pallas_call(
        _sb_decode_kernel,
        grid_spec=grid_spec,
        out_shape=jax.ShapeDtypeStruct((bs, r, SB_WIDTH), F32),
        compiler_params=_params("parallel", "arbitrary"),
        name="sb_decode",
    )(pt_flat, qbd, bias_rows, pad_new(k_new), pad_new(v_new), *([cache_k] * n), *([cache_v] * n))
    return o[:, :ts].reshape(bs * ts, SB_WIDTH).astype(BF16)


def _ret_kernel(*refs, has_s0):
    if has_s0:
        q_ref, k_ref, v_ref, g_ref, gng_ref, gnb_ref, dm_ref, qd_ref, kd_ref, cd_ref, s0_ref, y_ref, so_ref, s_ref = refs
    else:
        q_ref, k_ref, v_ref, g_ref, gng_ref, gnb_ref, dm_ref, qd_ref, kd_ref, cd_ref, y_ref, so_ref, s_ref = refs
    c = pl.program_id(2)

    @pl.when(c == 0)
    def _():
        if has_s0:
            s_ref[...] = s0_ref[0, 0]
        else:
            s_ref[...] = jnp.zeros_like(s_ref)

    q = q_ref[...]
    k = k_ref[...]
    v = v_ref[...]
    s = s_ref[...]
    inner = _dot_nt(q, k.astype(BF16)) * dm_ref[0]
    o = _dot(inner.astype(BF16), v) + _dot(q, s.astype(BF16)) * qd_ref[0]
    s_new = cd_ref[0] * s + _dot_tn((k * kd_ref[0]).astype(BF16), v)
    s_ref[...] = s_new

    mu = jnp.mean(o, axis=-1, keepdims=True)
    d = o - mu
    var = jnp.mean(d * d, axis=-1, keepdims=True)
    y = d * lax.rsqrt(var + GN_EPS) * gng_ref[...] + gnb_ref[...]
    g = g_ref[...]
    y_ref[...] = (g * jax.nn.sigmoid(g) * y).astype(y_ref.dtype)

    @pl.when(c == pl.num_programs(2) - 1)
    def _():
        so_ref[0, 0] = s_new


def _retention_tables(c_real, c_pad):
    lg = jnp.log1p(-jnp.exp2(-5.0 - jnp.arange(RET_HEADS, dtype=F32)))
    idx = jnp.arange(c_pad, dtype=F32)
    real = idx < c_real
    diff = idx[:, None] - idx[None, :]
    dmask = jnp.where(diff >= 0, jnp.exp(lg[:, None, None] * jnp.maximum(diff, 0.0)), 0.0)
    dmask = jnp.where(real[None, :, None] & real[None, None, :], dmask, 0.0)
    qdec = jnp.exp(lg[:, None] * (idx[None, :] + 1.0))
    kdec = jnp.where(real[None, :], jnp.exp(lg[:, None] * (c_real - 1.0 - idx[None, :])), 0.0)
    cdec = jnp.exp(lg * c_real)
    wide = lambda a: jnp.broadcast_to(a[:, :, None], (RET_HEADS, c_pad, RET_HEAD_DIM))
    return dmask, wide(qdec), wide(kdec), jnp.broadcast_to(cdec[:, None, None], (RET_HEADS, 1, RET_HEAD_DIM))


def _retention(q, k, v, gsw, gn_g, gn_b, s0, batch, n_chunks, c_real, c_pad):
    dmask, qdec, kdec, cdec = _retention_tables(c_real, c_pad)
    d = RET_HEAD_DIM
    row_spec = pl.BlockSpec((c_pad, d), lambda b, h, c: (b * n_chunks + c, h))
    head_vec = pl.BlockSpec((1, d), lambda b, h, c: (0, h))
    table = lambda r: pl.BlockSpec((1, r, d), lambda b, h, c: (h, 0, 0))
    state_spec = pl.BlockSpec((1, 1, d, d), lambda b, h, c: (b, h, 0, 0))
    in_specs = [row_spec, row_spec, row_spec, row_spec, head_vec, head_vec,
                pl.BlockSpec((1, c_pad, c_pad), lambda b, h, c: (h, 0, 0)), table(c_pad), table(c_pad), table(1)]
    args = [q, k, v, gsw, gn_g.reshape(1, -1).astype(F32), gn_b.reshape(1, -1).astype(F32), dmask, qdec, kdec, cdec]
    if s0 is not None:
        in_specs.append(state_spec)
        args.append(s0)
    return pl.pallas_call(
        functools.partial(_ret_kernel, has_s0=s0 is not None),
        grid=(batch, RET_HEADS, n_chunks),
        in_specs=in_specs,
        out_specs=[row_spec, state_spec],
        out_shape=[jax.ShapeDtypeStruct((batch * n_chunks * c_pad, RET_WIDTH), BF16),
                   jax.ShapeDtypeStruct((batch, RET_HEADS, d, d), F32)],
        scratch_shapes=[pltpu.VMEM((d, d), F32)],
        compiler_params=_params("parallel", "parallel", "arbitrary"),
        name="retention",
    )(*args)


def _merge_kernel(oa_ref, yb_ref, oas_ref, ybs_ref, wa_ref, wb_ref, ga_ref, gb_ref, o_ref, *, n_full, tail):
    def body(rows):
        oa, yb = (oa_ref, yb_ref) if rows == slice(None) else (oas_ref, ybs_ref)
        a = _dot(oa[...], wa_ref[...])
        b = _dot(yb[...], wb_ref[...])
        o_ref[rows, :] = (jax.nn.sigmoid(ga_ref[rows, :]) * a
                          + jax.nn.sigmoid(gb_ref[rows, :]) * b).astype(o_ref.dtype)
    _for_rows(1, n_full, tail, body)


def _merge(oa, yb, oa_s, yb_s, wa, wb, layer, gates, tm, tn=COL_TILE_SMALL):
    k = oa.shape[1]
    m = gates.shape[0]
    n_full, tail = _row_steps(m, tm)
    assert oa.shape[0] == n_full * tm and oa_s.shape[0] == tail
    gb_off = D_MODEL // tn
    last_full = n_full - 1
    prompt_rows = pl.BlockSpec((tm, k), lambda n, i: (jnp.minimum(i, last_full), 0))
    sample_rows = pl.BlockSpec((tail, k), lambda n, i: (0, 0))
    return pl.pallas_call(
        functools.partial(_merge_kernel, n_full=n_full, tail=tail),
        grid=(D_MODEL // tn, pl.cdiv(m, tm)),
        in_specs=[prompt_rows, prompt_rows, sample_rows, sample_rows,
                  pl.BlockSpec((None, k, tn), lambda n, i: (layer, 0, n)),
                  pl.BlockSpec((None, k, tn), lambda n, i: (layer, 0, n)),
                  pl.BlockSpec((tm, tn), lambda n, i: (i, n)),
                  pl.BlockSpec((tm, tn), lambda n, i: (i, n + gb_off))],
        out_specs=pl.BlockSpec((tm, tn), lambda n, i: (i, n)),
        out_shape=jax.ShapeDtypeStruct((m, D_MODEL), BF16),
        compiler_params=_params("parallel", "parallel"),
        name="merge",
    )(oa, yb, oa_s, yb_s, wa, wb, gates, gates)


def _resid_kernel(a_ref, w_ref, x_ref, o_ref, *, n_full, tail):
    def body(rows):
        o_ref[rows, :] = ALPHA * x_ref[rows, :] + _dot(a_ref[rows, :], w_ref[...])
    _for_rows(1, n_full, tail, body)


def _resid_proj(a, w, layer, x, tm, tn=COL_TILE_SMALL):
    m, k = a.shape
    n_full, tail = _row_steps(m, tm)
    return pl.pallas_call(
        functools.partial(_resid_kernel, n_full=n_full, tail=tail),
        grid=(D_MODEL // tn, pl.cdiv(m, tm)),
        in_specs=[pl.BlockSpec((tm, k), lambda n, i: (i, 0)),
                  pl.BlockSpec((None, k, tn), lambda n, i: (layer, 0, n)),
                  pl.BlockSpec((tm, tn), lambda n, i: (i, n))],
        out_specs=pl.BlockSpec((tm, tn), lambda n, i: (i, n)),
        out_shape=jax.ShapeDtypeStruct((m, D_MODEL), F32),
        compiler_params=_params("parallel", "parallel"),
        name="resid_proj",
    )(a, w, x)


def _ln_kernel(x_ref, g_ref, b_ref, o_ref, obf_ref, *, n_full, tail):
    def body(rows):
        x = x_ref[rows, :]
        mu = jnp.mean(x, axis=-1, keepdims=True)
        d = x - mu
        var = jnp.mean(d * d, axis=-1, keepdims=True)
        y = d * lax.rsqrt(var + LN_EPS) * g_ref[...] + b_ref[...]
        o_ref[rows, :] = y
        obf_ref[rows, :] = y.astype(BF16)
    _for_rows(0, n_full, tail, body)


def _layer_norm(x, g, b, tm):
    m, n = x.shape
    n_full, tail = _row_steps(m, tm)
    row = pl.BlockSpec((tm, n), lambda i: (i, 0))
    vec = pl.BlockSpec((1, n), lambda i: (0, 0))
    return pl.pallas_call(
        functools.partial(_ln_kernel, n_full=n_full, tail=tail),
        grid=(pl.cdiv(m, tm),),
        in_specs=[row, vec, vec],
        out_specs=[row, row],
        out_shape=[jax.ShapeDtypeStruct((m, n), F32), jax.ShapeDtypeStruct((m, n), BF16)],
        compiler_params=_params("parallel"),
        name="layer_norm",
    )(x, g.reshape(1, n).astype(F32), b.reshape(1, n).astype(F32))


def _swiglu_kernel(x_ref, wg_ref, wu_ref, o_ref, *, n_full, tail):
    def body(rows):
        x = x_ref[rows, :]
        g = _dot(x, wg_ref[...])
        u = _dot(x, wu_ref[...])
        o_ref[rows, :] = (g * jax.nn.sigmoid(g) * u).astype(o_ref.dtype)
    _for_rows(1, n_full, tail, body)


def _swiglu(x, w, layer, tm, tn=MXU_WIDTH):
    m, k = x.shape
    n_full, tail = _row_steps(m, tm)
    up_off = D_FF // tn
    return pl.pallas_call(
        functools.partial(_swiglu_kernel, n_full=n_full, tail=tail),
        grid=(D_FF // tn, pl.cdiv(m, tm)),
        in_specs=[pl.BlockSpec((tm, k), lambda n, i: (i, 0)),
                  pl.BlockSpec((None, k, tn), lambda n, i: (layer, 0, n)),
                  pl.BlockSpec((None, k, tn), lambda n, i: (layer, 0, n + up_off))],
        out_specs=pl.BlockSpec((tm, tn), lambda n, i: (i, n)),
        out_shape=jax.ShapeDtypeStruct((m, D_FF), BF16),
        compiler_params=_params("parallel", "parallel"),
        name="swiglu",
    )(x, w, w)


def _ple_kernel(xbf_ref, wg_ref, p_ref, wp_ref, x_ref, o_ref, *, n_full, tail):
    def body(rows):
        gate = jax.nn.sigmoid(_dot(xbf_ref[rows, :], wg_ref[...]))
        o_ref[rows, :] = ALPHA * x_ref[rows, :] + gate * _dot(p_ref[rows, :], wp_ref[...])
    _for_rows(1, n_full, tail, body)


def _ple(xbf, wg, p, wp, layer, x, tm, tn=COL_TILE_SMALL):
    m, k = xbf.shape
    n_full, tail = _row_steps(m, tm)
    return pl.pallas_call(
        functools.partial(_ple_kernel, n_full=n_full, tail=tail),
        grid=(D_MODEL // tn, pl.cdiv(m, tm)),
        in_specs=[pl.BlockSpec((tm, k), lambda n, i: (i, 0)),
                  pl.BlockSpec((None, k, tn), lambda n, i: (layer, 0, n)),
                  pl.BlockSpec((None, tm, PLE_DIM), lambda n, i: (layer, i, 0)),
                  pl.BlockSpec((None, PLE_DIM, tn), lambda n, i: (layer, 0, n)),
                  pl.BlockSpec((tm, tn), lambda n, i: (i, n))],
        out_specs=pl.BlockSpec((tm, tn), lambda n, i: (i, n)),
        out_shape=jax.ShapeDtypeStruct((m, D_MODEL), F32),
        compiler_params=_params("parallel", "parallel"),
        name="ple",
    )(xbf, wg, p, wp, x)


def _ffn_out_kernel(a_ref, w_ref, r_ref, o_ref, *, n_full, tail):
    k = pl.program_id(2)

    def body(rows):
        d = _dot(a_ref[rows, :], w_ref[...])

        @pl.when(k == 0)
        def _():
            o_ref[rows, :] = r_ref[rows, :] + d

        @pl.when(k > 0)
        def _():
            o_ref[rows, :] += d

    _for_rows(1, n_full, tail, body)


def _ffn_out(a, w, layer, r, tm, tn=COL_TILE_SMALL, nk=2):
    m, k = a.shape
    n_full, tail = _row_steps(m, tm)
    tk = k // nk
    return pl.pallas_call(
        functools.partial(_ffn_out_kernel, n_full=n_full, tail=tail),
        grid=(D_MODEL // tn, pl.cdiv(m, tm), nk),
        in_specs=[pl.BlockSpec((tm, tk), lambda n, i, j: (i, j)),
                  pl.BlockSpec((None, tk, tn), lambda n, i, j: (layer, j, n)),
                  pl.BlockSpec((tm, tn), lambda n, i, j: (i, n))],
        out_specs=pl.BlockSpec((tm, tn), lambda n, i, j: (i, n)),
        out_shape=jax.ShapeDtypeStruct((m, D_MODEL), F32),
        compiler_params=_params("parallel", "parallel", "arbitrary"),
        name="ffn_out",
    )(a, w, r)


def _rotary_tables(pos):
    theta = ROPE_BASE ** (-jnp.arange(0, RET_HEAD_DIM, 2, dtype=F32) / RET_HEAD_DIM)
    ang = pos.astype(F32)[:, None] * theta[None, :]
    return jnp.cos(ang), jnp.sin(ang)


def _pad_rows(a, batch, t, t_pad):
    return jnp.pad(a.reshape(batch, t, -1), ((0, 0), (0, t_pad - t), (0, 0))).reshape(batch * t_pad, -1)


def kernel(x_prompt, x_sample, cache_sb_k, cache_sb_v, state_ret, page_table, p_prompt, p_sample,
           w_in, b_sb, gn_g, gn_b, w_proj_a, w_proj_b, w_out, ln1_g, ln1_b,
           w_ffn_in, w_ffn_out, w_ple_gate, w_ple_proj, ln2_g, ln2_b):
    bp, tp, _ = x_prompt.shape
    bs, ts, _ = x_sample.shape
    mp, ms = bp * tp, bs * ts
    n_pool = cache_sb_k.shape[1]
    past = page_table.shape[1] * PAGE_SIZE
    cache_k = cache_sb_k.reshape(DEPTH, n_pool, PAGE_SIZE * SB_HEADS, SB_HEAD_DIM)
    cache_v = cache_sb_v.reshape(DEPTH, n_pool, PAGE_SIZE * SB_HEADS, SB_HEAD_DIM)

    pos = jnp.concatenate([jnp.tile(jnp.arange(tp), bp), jnp.tile(past + jnp.arange(ts), bs)])
    cos, sin = _rotary_tables(pos)
    ts_pad = 16
    pad = lambda a: _pad_rows(a[mp:], bs, ts, ts_pad)

    x = jnp.concatenate([x_prompt.reshape(mp, D_MODEL), x_sample.reshape(ms, D_MODEL)], axis=0)
    xbf = x.astype(BF16)
    p_all = jnp.concatenate([p_prompt.reshape(DEPTH, mp, PLE_DIM), p_sample.reshape(DEPTH, ms, PLE_DIM)],
                            axis=1).astype(BF16)
    w_in, w_proj_a, w_proj_b, w_out, w_ffn_in, w_ffn_out, w_ple_gate, w_ple_proj = (
        w.astype(BF16) for w in (w_in, w_proj_a, w_proj_b, w_out, w_ffn_in, w_ffn_out, w_ple_gate, w_ple_proj))
    kp = jnp.zeros((DEPTH, mp * SB_HEADS, SB_HEAD_DIM), F32)
    vp = jnp.zeros((DEPTH, mp * SB_HEADS, SB_HEAD_DIM), F32)
    ks = jnp.zeros((DEPTH, ms * SB_HEADS, SB_HEAD_DIM), F32)
    vs = jnp.zeros((DEPTH, ms * SB_HEADS, SB_HEAD_DIM), F32)
    sp_l, ss_l = [], []
    tm = min(ROW_TILE, mp)
    tm_wide = min(2 * ROW_TILE, mp)
    tm_norm = min(256, mp)
    for i in range(DEPTH):
        kp, ks, k_a = _kv_into_slots(kp, ks, xbf, w_in, OFF_KA, i, tm)
        vp, vs, v_a = _kv_into_slots(vp, vs, xbf, w_in, OFF_VA, i, tm)
        q_a = _proj(xbf, w_in, i, OFF_QA, SB_WIDTH, BF16, tm)
        q_b = _proj_rotary(xbf, w_in, i, OFF_QB, cos, sin, 1.0, BF16, tm)
        k_b = _proj_rotary(xbf, w_in, i, OFF_KB, cos, sin, RET_HEAD_DIM ** -0.5, F32, tm)
        v_b = _proj(xbf, w_in, i, OFF_VB, RET_WIDTH, BF16, tm)
        gsw = _proj(xbf, w_in, i, OFF_GSW, RET_WIDTH, F32, tm)
        gates = _proj(xbf, w_in, i, OFF_G, 2 * D_MODEL, F32, tm)

        oa_p = _sb_prompt(q_a, k_a, v_a, b_sb[i], bp, tp)
        yb_p, s_p = _retention(q_b, k_b, v_b, gsw, gn_g[i], gn_b[i], None, bp, tp // RET_BLOCK, RET_BLOCK, RET_BLOCK)
        oa_s = _sb_decode(q_a[mp:], k_a[mp:], v_a[mp:], cache_k, cache_v, page_table, b_sb[i], i)
        yb_s, s_s = _retention(pad(q_b), pad(k_b), pad(v_b), pad(gsw), gn_g[i], gn_b[i], state_ret[i],
                               bs, 1, ts, ts_pad)
        yb_s = yb_s.reshape(bs, ts_pad, RET_WIDTH)[:, :ts].reshape(ms, RET_WIDTH)
        sp_l.append(s_p)
        ss_l.append(s_s)

        merged = _merge(oa_p, yb_p, oa_s, yb_s, w_proj_a, w_proj_b, i, gates, tm)
        x1, x1bf = _layer_norm(_resid_proj(merged, w_out, i, x, tm), ln1_g[i], ln1_b[i], tm_norm)
        act = _swiglu(x1bf, w_ffn_in, i, tm_wide)
        side = _ple(x1bf, w_ple_gate, p_all, w_ple_proj, i, x1, tm)
        x, xbf = _layer_norm(_ffn_out(act, w_ffn_out, i, side, tm), ln2_g[i], ln2_b[i], tm_norm)

    heads = lambda a, b, t: a.reshape(DEPTH, b, t, SB_HEADS, SB_HEAD_DIM)
    return (x[:mp].reshape(bp, tp, D_MODEL), x[mp:].reshape(bs, ts, D_MODEL),
            heads(kp, bp, tp), heads(vp, bp, tp), heads(ks, bs, ts), heads(vs, bs, ts),
            jnp.stack(sp_l), jnp.stack(ss_l))
```

```python
import functools

import jax
import jax.numpy as jnp
from jax import lax
from jax.experimental import pallas as pl
from jax.experimental.pallas import tpu as pltpu

F32 = jnp.float32
BF16 = jnp.bfloat16

D_MODEL = 4096
DEPTH = 4
PAGE_SIZE = 128
SB_HEADS = 16
SB_HEAD_DIM = 128
SB_WIDTH = SB_HEADS * SB_HEAD_DIM
RET_HEADS = 8
RET_HEAD_DIM = 256
RET_WIDTH = RET_HEADS * RET_HEAD_DIM
D_FF = 11008
PLE_DIM = 256
ALPHA = (2 * DEPTH) ** 0.25
LN_EPS = 1e-5
GN_EPS = 1e-6
ROPE_BASE = 10000.0

OFF_QA, OFF_KA, OFF_VA, OFF_QB, OFF_KB, OFF_VB, OFF_GSW, OFF_G = 0, 2048, 4096, 6144, 8192, 10240, 12288, 14336

V7X_VMEM_LIMIT_BYTES = 56 * 1024 * 1024
MXU_WIDTH = 256
ROW_TILE = 1024
COL_TILE = 1024
COL_TILE_SMALL = 512
SB_TILE = 512
SB_SUB = MXU_WIDTH
RET_BLOCK = MXU_WIDTH
DEC_ROWS_PER_HEAD = 8
DEC_PAGES_PER_STEP = SB_TILE // PAGE_SIZE


def _params(*sem):
    return pltpu.CompilerParams(dimension_semantics=sem, vmem_limit_bytes=V7X_VMEM_LIMIT_BYTES)


def _dot(a, b):
    return jnp.dot(a, b, preferred_element_type=F32)


def _dot_nt(a, b):
    return lax.dot_general(a, b, (((1,), (1,)), ((), ())), preferred_element_type=F32)


def _dot_tn(a, b):
    return lax.dot_general(a, b, (((0,), (0,)), ((), ())), preferred_element_type=F32)


def _on_last_step(axis):
    return pl.when(pl.program_id(axis) == pl.num_programs(axis) - 1)


def _proj_kernel(x_ref, xs_ref, w_ref, o_ref, os_ref):
    w = w_ref[...]
    o_ref[...] = _dot(x_ref[...], w).astype(o_ref.dtype)

    @_on_last_step(1)
    def _():
        os_ref[...] = _dot(xs_ref[...], w).astype(os_ref.dtype)


def _proj(x, xs, w, layer, col_off, n_cols, out_dtype, tm, tn=COL_TILE):
    m, k = x.shape
    ms = xs.shape[0]
    off = col_off // tn
    return pl.pallas_call(
        _proj_kernel,
        grid=(n_cols // tn, m // tm),
        in_specs=[pl.BlockSpec((tm, k), lambda n, i: (i, 0)),
                  pl.BlockSpec((ms, k), lambda n, i: (0, 0)),
                  pl.BlockSpec((None, k, tn), lambda n, i: (layer, 0, n + off))],
        out_specs=[pl.BlockSpec((tm, tn), lambda n, i: (i, n)),
                   pl.BlockSpec((ms, tn), lambda n, i: (0, n))],
        out_shape=[jax.ShapeDtypeStruct((m, n_cols), out_dtype), jax.ShapeDtypeStruct((ms, n_cols), out_dtype)],
        compiler_params=_params("parallel", "arbitrary"),
        name="proj",
    )(x, xs, w)


def _proj_rows_kernel(x_ref, w_ref, o_ref):
    o_ref[...] = _dot(x_ref[...], w_ref[...]).astype(o_ref.dtype)


def _proj_rows(x, w, layer, col_off, n_cols, out_dtype, tn=COL_TILE):
    m, k = x.shape
    off = col_off // tn
    return pl.pallas_call(
        _proj_rows_kernel,
        grid=(n_cols // tn,),
        in_specs=[pl.BlockSpec((m, k), lambda n: (0, 0)),
                  pl.BlockSpec((None, k, tn), lambda n: (layer, 0, n + off))],
        out_specs=pl.BlockSpec((m, tn), lambda n: (0, n)),
        out_shape=jax.ShapeDtypeStruct((m, n_cols), out_dtype),
        compiler_params=_params("parallel"),
        name="proj_rows",
    )(x, w)


def _kv_slot_kernel(x_ref, w_ref, buf_ref, o_ref, obf_ref):
    del buf_ref
    n = pl.program_id(1)
    heads_per_tile = obf_ref.shape[1] // SB_HEAD_DIM
    acc = _dot(x_ref[...], w_ref[...])
    obf_ref[...] = acc.astype(BF16)
    for j in range(heads_per_tile):
        o_ref[0, pl.ds(n * heads_per_tile + j, x_ref.shape[0], stride=SB_HEADS), :] = (
            acc[:, j * SB_HEAD_DIM:(j + 1) * SB_HEAD_DIM])


def _kv_into_slot(buf, x, w, col_off, layer, tm, tn=COL_TILE_SMALL):
    m, k = x.shape
    off = col_off // tn
    return pl.pallas_call(
        _kv_slot_kernel,
        grid=(m // tm, SB_WIDTH // tn),
        in_specs=[pl.BlockSpec((tm, k), lambda i, n: (i, 0)),
                  pl.BlockSpec((None, k, tn), lambda i, n: (layer, 0, n + off)),
                  pl.BlockSpec(memory_space=pl.ANY)],
        out_specs=[pl.BlockSpec((1, tm * SB_HEADS, SB_HEAD_DIM), lambda i, n: (layer, i, 0)),
                   pl.BlockSpec((tm, tn), lambda i, n: (i, n))],
        out_shape=[jax.ShapeDtypeStruct(buf.shape, F32), jax.ShapeDtypeStruct((m, SB_WIDTH), BF16)],
        input_output_aliases={2: 0},
        compiler_params=_params("parallel", "arbitrary"),
        name="kv_slot",
    )(x, w, buf)


def _rotate(acc, c, s, scale, o_ref):
    half = RET_HEAD_DIM // 2
    for j in range(acc.shape[1] // RET_HEAD_DIM):
        lo = j * RET_HEAD_DIM
        x1 = acc[:, lo:lo + half]
        x2 = acc[:, lo + half:lo + RET_HEAD_DIM]
        o_ref[:, lo:lo + half] = ((x1 * c - x2 * s) * scale).astype(o_ref.dtype)
        o_ref[:, lo + half:lo + RET_HEAD_DIM] = ((x1 * s + x2 * c) * scale).astype(o_ref.dtype)


def _rot_kernel(x_ref, xs_ref, w_ref, cos_ref, sin_ref, coss_ref, sins_ref, o_ref, os_ref, *, scale):
    w = w_ref[...]
    _rotate(_dot(x_ref[...], w), cos_ref[...], sin_ref[...], scale, o_ref)

    @_on_last_step(1)
    def _():
        _rotate(_dot(xs_ref[...], w), coss_ref[...], sins_ref[...], scale, os_ref)


def _proj_rotary(x, xs, w, layer, col_off, tables, tables_s, scale, out_dtype, tm, tn=COL_TILE):
    m, k = x.shape
    ms = xs.shape[0]
    off = col_off // tn
    half = RET_HEAD_DIM // 2
    nrep = tables[0].shape[0] // tm
    table = pl.BlockSpec((tm, half), lambda n, i: (i % nrep, 0))
    table_s = pl.BlockSpec((ms, half), lambda n, i: (0, 0))
    return pl.pallas_call(
        functools.partial(_rot_kernel, scale=scale),
        grid=(RET_WIDTH // tn, m // tm),
        in_specs=[pl.BlockSpec((tm, k), lambda n, i: (i, 0)),
                  pl.BlockSpec((ms, k), lambda n, i: (0, 0)),
                  pl.BlockSpec((None, k, tn), lambda n, i: (layer, 0, n + off)),
                  table, table, table_s, table_s],
        out_specs=[pl.BlockSpec((tm, tn), lambda n, i: (i, n)),
                   pl.BlockSpec((ms, tn), lambda n, i: (0, n))],
        out_shape=[jax.ShapeDtypeStruct((m, RET_WIDTH), out_dtype), jax.ShapeDtypeStruct((ms, RET_WIDTH), out_dtype)],
        compiler_params=_params("parallel", "arbitrary"),
        name="proj_rotary",
    )(x, xs, w, *tables, *tables_s)


def _suffix_ones(n):
    row = lax.broadcasted_iota(jnp.int32, (2 * n, n), 0)
    col = lax.broadcasted_iota(jnp.int32, (2 * n, n), 1)
    return jnp.where((row % n) > col, 1.0, 0.0).astype(BF16)


def _sb_tile(q, k, v, bias, later, mask, sub):
    z = _dot_nt(q, k) * (SB_HEAD_DIM ** -0.5) + bias
    t = jnp.log(1.0 + jnp.exp(-jnp.abs(z)))
    softplus = jnp.maximum(z, 0.0) + t
    if mask is not None:
        softplus = jnp.where(mask, softplus, 0.0)
    log_beta = jnp.minimum(z, 0.0) - t
    ones_after = _suffix_ones(sub)
    parts = []
    for j in reversed(range(k.shape[0] // sub)):
        cols = slice(j * sub, (j + 1) * sub)
        s = softplus[:, cols]
        hi = s.astype(BF16)
        lo = (s - hi.astype(F32)).astype(BF16)
        after = _dot(jnp.concatenate([hi, lo], axis=1), ones_after) + later
        parts.append(jnp.exp(log_beta[:, cols] - after))
        later = later + jnp.sum(s, axis=1, keepdims=True)
    a = parts[0] if len(parts) == 1 else jnp.concatenate(parts[::-1], axis=1)
    if mask is not None:
        a = jnp.where(mask, a, 0.0)
    return _dot(a.astype(BF16), v), later


def _sb_prompt_kernel(bias_ref, q_ref, k_ref, v_ref, o_ref, acc_ref):
    qi = pl.program_id(2)
    tile = SB_TILE
    q = q_ref[...]
    bias = bias_ref[0]

    def step(kt, later, mask):
        start = pl.multiple_of(kt * tile, tile)
        return _sb_tile(q, k_ref[pl.ds(start, tile), :], v_ref[pl.ds(start, tile), :], bias, later, mask, SB_SUB)

    row = lax.broadcasted_iota(jnp.int32, (tile, tile), 0)
    col = lax.broadcasted_iota(jnp.int32, (tile, tile), 1)
    o, later = step(qi, jnp.zeros((tile, 1), F32), col < row)
    acc_ref[...] = o

    def body(j, later):
        o, later = step(qi - 1 - j, later, None)
        acc_ref[...] += o
        return later

    lax.fori_loop(0, qi, body, later)
    o_ref[...] = acc_ref[...].astype(o_ref.dtype)


def _sb_prompt(q, k, v, bias, batch, seq):
    tile = SB_TILE
    nq = seq // tile
    bias_b = jnp.broadcast_to(bias.astype(F32)[:, None, None], (SB_HEADS, 1, tile))
    kv_spec = pl.BlockSpec((seq, SB_HEAD_DIM), lambda b, h, i: (b, h))
    return pl.pallas_call(
        _sb_prompt_kernel,
        grid=(batch, SB_HEADS, nq),
        in_specs=[pl.BlockSpec((1, 1, tile), lambda b, h, i: (h, 0, 0)),
                  pl.BlockSpec((tile, SB_HEAD_DIM), lambda b, h, i: (b * nq + i, h)),
                  kv_spec, kv_spec],
        out_specs=pl.BlockSpec((tile, SB_HEAD_DIM), lambda b, h, i: (b * nq + i, h)),
        out_shape=jax.ShapeDtypeStruct((batch * seq, SB_WIDTH), BF16),
        scratch_shapes=[pltpu.VMEM((tile, SB_HEAD_DIM), F32)],
        compiler_params=_params("parallel", "parallel", "arbitrary"),
        name="sb_prompt",
    )(bias_b, q, k, v)


def _heads_to_lanes(page_ref):
    return jnp.concatenate(
        [page_ref[0, 0, pl.ds(h, PAGE_SIZE, stride=SB_HEADS), :].astype(BF16) for h in range(SB_HEADS)], axis=1)


def _sb_decode_kernel(pt_ref, q_ref, bias_ref, kn_ref, vn_ref, *refs):
    del pt_ref
    n = DEC_PAGES_PER_STEP
    k_refs, v_refs = refs[:n], refs[n:2 * n]
    o_ref, acc_ref, later_ref = refs[2 * n:]
    p = pl.program_id(1)
    rows = SB_HEADS * DEC_ROWS_PER_HEAD
    q = q_ref[0]
    bias = bias_ref[...]

    @pl.when(p == 0)
    def _():
        row = lax.broadcasted_iota(jnp.int32, (rows, PAGE_SIZE), 0)
        col = lax.broadcasted_iota(jnp.int32, (rows, PAGE_SIZE), 1)
        o, later = _sb_tile(q, kn_ref[0], vn_ref[0], bias,
                            jnp.zeros((rows, 1), F32), col < (row % DEC_ROWS_PER_HEAD), PAGE_SIZE)
        acc_ref[...] = o
        later_ref[...] = later

    k = jnp.concatenate([_heads_to_lanes(r) for r in k_refs[::-1]], axis=0)
    v = jnp.concatenate([_heads_to_lanes(r) for r in v_refs[::-1]], axis=0)
    o, later = _sb_tile(q, k, v, bias, later_ref[...], None, SB_SUB)
    acc_ref[...] += o
    later_ref[...] = later

    @pl.when(p == pl.num_programs(1) - 1)
    def _():
        r = DEC_ROWS_PER_HEAD
        for h in range(SB_HEADS):
            cols = slice(h * SB_HEAD_DIM, (h + 1) * SB_HEAD_DIM)
            o_ref[0, :, cols] = acc_ref[h * r:(h + 1) * r, cols]


def _sb_decode(q, k_new, v_new, cache_k, cache_v, page_table, bias, layer):
    bs, n_pages = page_table.shape
    ts = q.shape[0] // bs
    r = DEC_ROWS_PER_HEAD
    rows = SB_HEADS * r
    n = DEC_PAGES_PER_STEP
    q4 = jnp.pad(q.reshape(bs, ts, SB_HEADS, SB_HEAD_DIM), ((0, 0), (0, r - ts), (0, 0), (0, 0)))
    q4 = q4.transpose(0, 2, 1, 3)
    eye = jnp.eye(SB_HEADS, dtype=jnp.bool_)[None, :, None, :, None]
    qbd = jnp.where(eye, q4[:, :, :, None, :], jnp.zeros((), q.dtype)).reshape(bs, rows, SB_WIDTH)
    pad_new = lambda a: jnp.pad(a.reshape(bs, ts, SB_WIDTH), ((0, 0), (0, PAGE_SIZE - ts), (0, 0)))
    bias_rows = jnp.repeat(bias.astype(F32), r)[:, None]
    pt_flat = page_table.reshape(-1)

    def page_spec(slot):
        def page_map(b, p, pt):
            return (layer, pt[b * n_pages + (n_pages - 1 - (p * n + slot))], 0, 0)
        return pl.BlockSpec((1, 1, PAGE_SIZE * SB_HEADS, SB_HEAD_DIM), page_map)

    new_spec = pl.BlockSpec((1, PAGE_SIZE, SB_WIDTH), lambda b, p, pt: (b, 0, 0))
    grid_spec = pltpu.PrefetchScalarGridSpec(
        num_scalar_prefetch=1,
        grid=(bs, n_pages // n),
        in_specs=[pl.BlockSpec((1, rows, SB_WIDTH), lambda b, p, pt: (b, 0, 0)),
                  pl.BlockSpec((rows, 1), lambda b, p, pt: (0, 0)),
                  new_spec, new_spec] + [page_spec(s) for s in range(n)] * 2,
        out_specs=pl.BlockSpec((1, r, SB_WIDTH), lambda b, p, pt: (b, 0, 0)),
        scratch_shapes=[pltpu.VMEM((rows, SB_WIDTH), F32), pltpu.VMEM((rows, 1), F32)],
    )
    o = pl.pallas_call(
        _sb_decode_kernel,
        grid_spec=grid_spec,
        out_shape=jax.ShapeDtypeStruct((bs, r, SB_WIDTH), F32),
        compiler_params=_params("parallel", "arbitrary"),
        name="sb_decode",
    )(pt_flat, qbd, bias_rows, pad_new(k_new), pad_new(v_new), *([cache_k] * n), *([cache_v] * n))
    return o[:, :ts].reshape(bs * ts, SB_WIDTH).astype(BF16)


def _ret_kernel(*refs, has_s0):
    if has_s0:
        q_ref, k_ref, v_ref, g_ref, gng_ref, gnb_ref, dm_ref, qd_ref, kd_ref, cd_ref, s0_ref, y_ref, so_ref, s_ref = refs
    else:
        q_ref, k_ref, v_ref, g_ref, gng_ref, gnb_ref, dm_ref, qd_ref, kd_ref, cd_ref, y_ref, so_ref, s_ref = refs
    c = pl.program_id(2)

    @pl.when(c == 0)
    def _():
        if has_s0:
            s_ref[...] = s0_ref[0, 0]
        else:
            s_ref[...] = jnp.zeros_like(s_ref)

    q = q_ref[...]
    k = k_ref[...]
    v = v_ref[...]
    s = s_ref[...]
    inner = _dot_nt(q, k.astype(BF16)) * dm_ref[0]
    o = _dot(inner.astype(BF16), v) + _dot(q, s.astype(BF16)) * qd_ref[0]
    s_new = cd_ref[0] * s + _dot_tn((k * kd_ref[0]).astype(BF16), v)
    s_ref[...] = s_new

    mu = jnp.mean(o, axis=-1, keepdims=True)
    d = o - mu
    var = jnp.mean(d * d, axis=-1, keepdims=True)
    y = d * lax.rsqrt(var + GN_EPS) * gng_ref[...] + gnb_ref[...]
    g = g_ref[...]
    y_ref[...] = (g * jax.nn.sigmoid(g) * y).astype(y_ref.dtype)

    @pl.when(c == pl.num_programs(2) - 1)
    def _():
        so_ref[0, 0] = s_new


def _retention_tables(c_real, c_pad):
    lg = jnp.log1p(-jnp.exp2(-5.0 - jnp.arange(RET_HEADS, dtype=F32)))
    idx = jnp.arange(c_pad, dtype=F32)
    real = idx < c_real
    diff = idx[:, None] - idx[None, :]
    dmask = jnp.where(diff >= 0, jnp.exp(lg[:, None, None] * jnp.maximum(diff, 0.0)), 0.0)
    dmask = jnp.where(real[None, :, None] & real[None, None, :], dmask, 0.0)
    qdec = jnp.exp(lg[:, None] * (idx[None, :] + 1.0))
    kdec = jnp.where(real[None, :], jnp.exp(lg[:, None] * (c_real - 1.0 - idx[None, :])), 0.0)
    cdec = jnp.exp(lg * c_real)
    wide = lambda a: jnp.broadcast_to(a[:, :, None], (RET_HEADS, c_pad, RET_HEAD_DIM))
    return dmask, wide(qdec), wide(kdec), jnp.broadcast_to(cdec[:, None, None], (RET_HEADS, 1, RET_HEAD_DIM))


def _retention(q, k, v, gsw, gn_g, gn_b, s0, batch, n_chunks, c_real, c_pad):
    dmask, qdec, kdec, cdec = _retention_tables(c_real, c_pad)
    d = RET_HEAD_DIM
    row_spec = pl.BlockSpec((c_pad, d), lambda b, h, c: (b * n_chunks + c, h))
    head_vec = pl.BlockSpec((1, d), lambda b, h, c: (0, h))
    table = lambda r: pl.BlockSpec((1, r, d), lambda b, h, c: (h, 0, 0))
    state_spec = pl.BlockSpec((1, 1, d, d), lambda b, h, c: (b, h, 0, 0))
    in_specs = [row_spec, row_spec, row_spec, row_spec, head_vec, head_vec,
                pl.BlockSpec((1, c_pad, c_pad), lambda b, h, c: (h, 0, 0)), table(c_pad), table(c_pad), table(1)]
    args = [q, k, v, gsw, gn_g.reshape(1, -1).astype(F32), gn_b.reshape(1, -1).astype(F32), dmask, qdec, kdec, cdec]
    if s0 is not None:
        in_specs.append(state_spec)
        args.append(s0)
    return pl.pallas_call(
        functools.partial(_ret_kernel, has_s0=s0 is not None),
        grid=(batch, RET_HEADS, n_chunks),
        in_specs=in_specs,
        out_specs=[row_spec, state_spec],
        out_shape=[jax.ShapeDtypeStruct(q.shape, BF16),
                   jax.ShapeDtypeStruct((batch, RET_HEADS, d, d), F32)],
        scratch_shapes=[pltpu.VMEM((d, d), F32)],
        compiler_params=_params("parallel", "parallel", "arbitrary"),
        name="retention",
    )(*args)


def _merge_kernel(oa_ref, yb_ref, oas_ref, ybs_ref, wa_ref, wb_ref, ga_ref, gb_ref, gas_ref, gbs_ref, o_ref, os_ref):
    wa = wa_ref[...]
    wb = wb_ref[...]

    def gated(oa, yb, ga, gb):
        return jax.nn.sigmoid(ga[...]) * _dot(oa[...], wa) + jax.nn.sigmoid(gb[...]) * _dot(yb[...], wb)

    o_ref[...] = gated(oa_ref, yb_ref, ga_ref, gb_ref).astype(o_ref.dtype)

    @_on_last_step(1)
    def _():
        os_ref[...] = gated(oas_ref, ybs_ref, gas_ref, gbs_ref).astype(os_ref.dtype)


def _merge(oa, yb, oa_s, yb_s, wa, wb, layer, gates, gates_s, tm, tn=COL_TILE_SMALL):
    m, k = oa.shape
    ms = oa_s.shape[0]
    gb_off = D_MODEL // tn
    rows = pl.BlockSpec((tm, k), lambda n, i: (i, 0))
    rows_s = pl.BlockSpec((ms, k), lambda n, i: (0, 0))
    weight = pl.BlockSpec((None, k, tn), lambda n, i: (layer, 0, n))
    return pl.pallas_call(
        _merge_kernel,
        grid=(D_MODEL // tn, m // tm),
        in_specs=[rows, rows, rows_s, rows_s, weight, weight,
                  pl.BlockSpec((tm, tn), lambda n, i: (i, n)),
                  pl.BlockSpec((tm, tn), lambda n, i: (i, n + gb_off)),
                  pl.BlockSpec((ms, tn), lambda n, i: (0, n)),
                  pl.BlockSpec((ms, tn), lambda n, i: (0, n + gb_off))],
        out_specs=[pl.BlockSpec((tm, tn), lambda n, i: (i, n)),
                   pl.BlockSpec((ms, tn), lambda n, i: (0, n))],
        out_shape=[jax.ShapeDtypeStruct((m, D_MODEL), BF16), jax.ShapeDtypeStruct((ms, D_MODEL), BF16)],
        compiler_params=_params("parallel", "arbitrary"),
        name="merge",
    )(oa, yb, oa_s, yb_s, wa, wb, gates, gates, gates_s, gates_s)


def _resid_kernel(a_ref, as_ref, w_ref, x_ref, xs_ref, o_ref, os_ref):
    w = w_ref[...]
    o_ref[...] = ALPHA * x_ref[...] + _dot(a_ref[...], w)

    @_on_last_step(1)
    def _():
        os_ref[...] = ALPHA * xs_ref[...] + _dot(as_ref[...], w)


def _resid_proj(a, a_s, w, layer, x, x_s, tm, tn=COL_TILE_SMALL):
    m, k = a.shape
    ms = a_s.shape[0]
    return pl.pallas_call(
        _resid_kernel,
        grid=(D_MODEL // tn, m // tm),
        in_specs=[pl.BlockSpec((tm, k), lambda n, i: (i, 0)),
                  pl.BlockSpec((ms, k), lambda n, i: (0, 0)),
                  pl.BlockSpec((None, k, tn), lambda n, i: (layer, 0, n)),
                  pl.BlockSpec((tm, tn), lambda n, i: (i, n)),
                  pl.BlockSpec((ms, tn), lambda n, i: (0, n))],
        out_specs=[pl.BlockSpec((tm, tn), lambda n, i: (i, n)),
                   pl.BlockSpec((ms, tn), lambda n, i: (0, n))],
        out_shape=[jax.ShapeDtypeStruct((m, D_MODEL), F32), jax.ShapeDtypeStruct((ms, D_MODEL), F32)],
        compiler_params=_params("parallel", "arbitrary"),
        name="resid_proj",
    )(a, a_s, w, x, x_s)


def _ln_kernel(x_ref, g_ref, b_ref, o_ref, obf_ref):
    x = x_ref[...]
    mu = jnp.mean(x, axis=-1, keepdims=True)
    d = x - mu
    var = jnp.mean(d * d, axis=-1, keepdims=True)
    y = d * lax.rsqrt(var + LN_EPS) * g_ref[...] + b_ref[...]
    o_ref[...] = y
    obf_ref[...] = y.astype(BF16)


def _layer_norm(x, g, b):
    m, n = x.shape
    tm = min(m, 256)
    row = pl.BlockSpec((tm, n), lambda i: (i, 0))
    vec = pl.BlockSpec((1, n), lambda i: (0, 0))
    return pl.pallas_call(
        _ln_kernel,
        grid=(m // tm,),
        in_specs=[row, vec, vec],
        out_specs=[row, row],
        out_shape=[jax.ShapeDtypeStruct((m, n), F32), jax.ShapeDtypeStruct((m, n), BF16)],
        compiler_params=_params("parallel"),
        name="layer_norm",
    )(x, g.reshape(1, n).astype(F32), b.reshape(1, n).astype(F32))


def _swiglu_kernel(x_ref, xs_ref, wg_ref, wu_ref, o_ref, os_ref):
    wg = wg_ref[...]
    wu = wu_ref[...]

    def act(x):
        g = _dot(x, wg)
        return g * jax.nn.sigmoid(g) * _dot(x, wu)

    o_ref[...] = act(x_ref[...]).astype(o_ref.dtype)

    @_on_last_step(1)
    def _():
        os_ref[...] = act(xs_ref[...]).astype(os_ref.dtype)


def _swiglu(x, xs, w, layer, tm, tn=MXU_WIDTH):
    m, k = x.shape
    ms = xs.shape[0]
    up_off = D_FF // tn
    return pl.pallas_call(
        _swiglu_kernel,
        grid=(D_FF // tn, m // tm),
        in_specs=[pl.BlockSpec((tm, k), lambda n, i: (i, 0)),
                  pl.BlockSpec((ms, k), lambda n, i: (0, 0)),
                  pl.BlockSpec((None, k, tn), lambda n, i: (layer, 0, n)),
                  pl.BlockSpec((None, k, tn), lambda n, i: (layer, 0, n + up_off))],
        out_specs=[pl.BlockSpec((tm, tn), lambda n, i: (i, n)),
                   pl.BlockSpec((ms, tn), lambda n, i: (0, n))],
        out_shape=[jax.ShapeDtypeStruct((m, D_FF), BF16), jax.ShapeDtypeStruct((ms, D_FF), BF16)],
        compiler_params=_params("parallel", "arbitrary"),
        name="swiglu",
    )(x, xs, w, w)


def _ple_kernel(xbf_ref, xbfs_ref, wg_ref, p_ref, ps_ref, wp_ref, x_ref, xs_ref, o_ref, os_ref):
    wg = wg_ref[...]
    wp = wp_ref[...]

    def side(xbf, p, x):
        return ALPHA * x[...] + jax.nn.sigmoid(_dot(xbf[...], wg)) * _dot(p[...], wp)

    o_ref[...] = side(xbf_ref, p_ref, x_ref)

    @_on_last_step(1)
    def _():
        os_ref[...] = side(xbfs_ref, ps_ref, xs_ref)


def _ple(xbf, xbf_s, wg, p, p_s, wp, layer, x, x_s, tm, tn=COL_TILE_SMALL):
    m, k = xbf.shape
    ms = xbf_s.shape[0]
    return pl.pallas_call(
        _ple_kernel,
        grid=(D_MODEL // tn, m // tm),
        in_specs=[pl.BlockSpec((tm, k), lambda n, i: (i, 0)),
                  pl.BlockSpec((ms, k), lambda n, i: (0, 0)),
                  pl.BlockSpec((None, k, tn), lambda n, i: (layer, 0, n)),
                  pl.BlockSpec((None, tm, PLE_DIM), lambda n, i: (layer, i, 0)),
                  pl.BlockSpec((None, ms, PLE_DIM), lambda n, i: (layer, 0, 0)),
                  pl.BlockSpec((None, PLE_DIM, tn), lambda n, i: (layer, 0, n)),
                  pl.BlockSpec((tm, tn), lambda n, i: (i, n)),
                  pl.BlockSpec((ms, tn), lambda n, i: (0, n))],
        out_specs=[pl.BlockSpec((tm, tn), lambda n, i: (i, n)),
                   pl.BlockSpec((ms, tn), lambda n, i: (0, n))],
        out_shape=[jax.ShapeDtypeStruct((m, D_MODEL), F32), jax.ShapeDtypeStruct((ms, D_MODEL), F32)],
        compiler_params=_params("parallel", "arbitrary"),
        name="ple",
    )(xbf, xbf_s, wg, p, p_s, wp, x, x_s)


def _ffn_out_kernel(a_ref, as_ref, w_ref, r_ref, rs_ref, o_ref, os_ref):
    k = pl.program_id(2)
    w = w_ref[...]

    def accumulate(a, r, o):
        d = _dot(a[...], w)

        @pl.when(k == 0)
        def _():
            o[...] = r[...] + d

        @pl.when(k > 0)
        def _():
            o[...] += d

    accumulate(a_ref, r_ref, o_ref)

    @_on_last_step(1)
    def _():
        accumulate(as_ref, rs_ref, os_ref)


def _ffn_out(a, a_s, w, layer, r, r_s, tm, tn=COL_TILE_SMALL, nk=2):
    m, k = a.shape
    ms = a_s.shape[0]
    tk = k // nk
    return pl.pallas_call(
        _ffn_out_kernel,
        grid=(D_MODEL // tn, m // tm, nk),
        in_specs=[pl.BlockSpec((tm, tk), lambda n, i, j: (i, j)),
                  pl.BlockSpec((ms, tk), lambda n, i, j: (0, j)),
                  pl.BlockSpec((None, tk, tn), lambda n, i, j: (layer, j, n)),
                  pl.BlockSpec((tm, tn), lambda n, i, j: (i, n)),
                  pl.BlockSpec((ms, tn), lambda n, i, j: (0, n))],
        out_specs=[pl.BlockSpec((tm, tn), lambda n, i, j: (i, n)),
                   pl.BlockSpec((ms, tn), lambda n, i, j: (0, n))],
        out_shape=[jax.ShapeDtypeStruct((m, D_MODEL), F32), jax.ShapeDtypeStruct((ms, D_MODEL), F32)],
        compiler_params=_params("parallel", "arbitrary", "arbitrary"),
        name="ffn_out",
    )(a, a_s, w, r, r_s)


def _rotary_tables(pos):
    theta = ROPE_BASE ** (-jnp.arange(0, RET_HEAD_DIM, 2, dtype=F32) / RET_HEAD_DIM)
    ang = pos.astype(F32)[:, None] * theta[None, :]
    return jnp.cos(ang), jnp.sin(ang)


def _pad_rows(a, batch, t, t_pad):
    return jnp.pad(a.reshape(batch, t, -1), ((0, 0), (0, t_pad - t), (0, 0))).reshape(batch * t_pad, -1)


def kernel(x_prompt, x_sample, cache_sb_k, cache_sb_v, state_ret, page_table, p_prompt, p_sample,
           w_in, b_sb, gn_g, gn_b, w_proj_a, w_proj_b, w_out, ln1_g, ln1_b,
           w_ffn_in, w_ffn_out, w_ple_gate, w_ple_proj, ln2_g, ln2_b):
    bp, tp, _ = x_prompt.shape
    bs, ts, _ = x_sample.shape
    mp, ms = bp * tp, bs * ts
    n_pool = cache_sb_k.shape[1]
    past = page_table.shape[1] * PAGE_SIZE
    cache_k = cache_sb_k.reshape(DEPTH, n_pool, PAGE_SIZE * SB_HEADS, SB_HEAD_DIM)
    cache_v = cache_sb_v.reshape(DEPTH, n_pool, PAGE_SIZE * SB_HEADS, SB_HEAD_DIM)

    rot_p = _rotary_tables(jnp.arange(tp))
    rot_s = _rotary_tables(jnp.tile(past + jnp.arange(ts), bs))
    ts_pad = 16
    pad = lambda a: _pad_rows(a, bs, ts, ts_pad)

    xp, xs = x_prompt.reshape(mp, D_MODEL), x_sample.reshape(ms, D_MODEL)
    xp_bf, xs_bf = xp.astype(BF16), xs.astype(BF16)
    pp = p_prompt.reshape(DEPTH, mp, PLE_DIM).astype(BF16)
    ps = p_sample.reshape(DEPTH, ms, PLE_DIM).astype(BF16)
    w_in, w_proj_a, w_proj_b, w_out, w_ffn_in, w_ffn_out, w_ple_gate, w_ple_proj = (
        w.astype(BF16) for w in (w_in, w_proj_a, w_proj_b, w_out, w_ffn_in, w_ffn_out, w_ple_gate, w_ple_proj))
    kp = jnp.zeros((DEPTH, mp * SB_HEADS, SB_HEAD_DIM), F32)
    vp = jnp.zeros((DEPTH, mp * SB_HEADS, SB_HEAD_DIM), F32)
    ks_l, vs_l, sp_l, ss_l = [], [], [], []
    tm = min(ROW_TILE, mp)
    tm_wide = min(2 * ROW_TILE, mp)
    for i in range(DEPTH):
        kp, k_a = _kv_into_slot(kp, xp_bf, w_in, OFF_KA, i, tm)
        vp, v_a = _kv_into_slot(vp, xp_bf, w_in, OFF_VA, i, tm)
        kv_s = _proj_rows(xs_bf, w_in, i, OFF_KA, 2 * SB_WIDTH, F32)
        q_a, q_a_s = _proj(xp_bf, xs_bf, w_in, i, OFF_QA, SB_WIDTH, BF16, tm)
        q_b, q_b_s = _proj_rotary(xp_bf, xs_bf, w_in, i, OFF_QB, rot_p, rot_s, 1.0, BF16, tm)
        k_b, k_b_s = _proj_rotary(xp_bf, xs_bf, w_in, i, OFF_KB, rot_p, rot_s, RET_HEAD_DIM ** -0.5, F32, tm)
        v_b, v_b_s = _proj(xp_bf, xs_bf, w_in, i, OFF_VB, RET_WIDTH, BF16, tm)
        gsw, gsw_s = _proj(xp_bf, xs_bf, w_in, i, OFF_GSW, RET_WIDTH, F32, tm)
        gates, gates_s = _proj(xp_bf, xs_bf, w_in, i, OFF_G, 2 * D_MODEL, F32, tm)

        oa = _sb_prompt(q_a, k_a, v_a, b_sb[i], bp, tp)
        yb, s_p = _retention(q_b, k_b, v_b, gsw, gn_g[i], gn_b[i], None, bp, tp // RET_BLOCK, RET_BLOCK, RET_BLOCK)
        k_s, v_s = kv_s[:, :SB_WIDTH], kv_s[:, SB_WIDTH:]
        oa_s = _sb_decode(q_a_s, k_s.astype(BF16), v_s.astype(BF16), cache_k, cache_v, page_table, b_sb[i], i)
        yb_s, s_s = _retention(pad(q_b_s), pad(k_b_s), pad(v_b_s), pad(gsw_s), gn_g[i], gn_b[i], state_ret[i],
                               bs, 1, ts, ts_pad)
        yb_s = yb_s.reshape(bs, ts_pad, RET_WIDTH)[:, :ts].reshape(ms, RET_WIDTH)
        ks_l.append(k_s)
        vs_l.append(v_s)
        sp_l.append(s_p)
        ss_l.append(s_s)

        merged, merged_s = _merge(oa, yb, oa_s, yb_s, w_proj_a, w_proj_b, i, gates, gates_s, tm)
        pre, pre_s = _resid_proj(merged, merged_s, w_out, i, xp, xs, tm)
        x1, x1_bf = _layer_norm(pre, ln1_g[i], ln1_b[i])
        x1_s, x1_s_bf = _layer_norm(pre_s, ln1_g[i], ln1_b[i])
        act, act_s = _swiglu(x1_bf, x1_s_bf, w_ffn_in, i, tm_wide)
        side, side_s = _ple(x1_bf, x1_s_bf, w_ple_gate, pp, ps, w_ple_proj, i, x1, x1_s, tm)
        pre, pre_s = _ffn_out(act, act_s, w_ffn_out, i, side, side_s, tm)
        xp, xp_bf = _layer_norm(pre, ln2_g[i], ln2_b[i])
        xs, xs_bf = _layer_norm(pre_s, ln2_g[i], ln2_b[i])

    heads = lambda a, b, t: a.reshape(DEPTH, b, t, SB_HEADS, SB_HEAD_DIM)
    return (xp.reshape(bp, tp, D_MODEL), xs.reshape(bs, ts, D_MODEL),
            heads(kp, bp, tp), heads(vp, bp, tp), heads(jnp.stack(ks_l), bs, ts), heads(jnp.stack(vs_l), bs, ts),
            jnp.stack(sp_l), jnp.stack(ss_l))
```
